```python
import math
import jax, jax.numpy as jnp
from jax import lax
import numpy as np

D_MODEL = 1024
BATCH = 8
SEQ = 8192
DEPTH = 2

N_MEM = 256
D_MIX = D_MODEL
POOL_WIDTH = D_MIX // 4
POOL_WINDOWS = (2, 4, 8, 16)
POOL_GROUP = POOL_WIDTH // 4
CONV_WIDTH = D_MIX // 4
CONV_KSIZE = 31
NSA_WIDTH = D_MIX - POOL_WIDTH - CONV_WIDTH
HEAD_DIM = 64
N_HEADS = NSA_WIDTH // HEAD_DIM
N_KV = 2
Q_PER_KV = N_HEADS // N_KV
KV_WIDTH = N_KV * HEAD_DIM
CMP_LEN = 32
CMP_STRIDE = 16
CMP_HIDDEN = 256
SEL_BLOCK = 64
N_SELECT = 16
WINDOW = 512
Q_BLOCK = 128
N_BRANCH = 3
N_BUCKETS = 32
MAX_EXACT = 16
MAX_DISTANCE = 128
XA_HEADS = 4
XA_HEAD_DIM = D_MODEL // XA_HEADS
D_FF = 4 * D_MODEL
EPS = 1e-6
NEG_INF = -1e30
FORCE = 1e30
IN_SPLITS = (POOL_WIDTH, 2 * CONV_WIDTH, NSA_WIDTH, 6 * KV_WIDTH, N_HEADS * N_BRANCH)
D_IN = POOL_WIDTH + 2 * CONV_WIDTH + NSA_WIDTH + 6 * KV_WIDTH + N_HEADS * N_BRANCH

kernel_name = 'hybrid_pool_conformer_nsa_decoder'


def rms_norm(x, g):
    xf = x.astype(jnp.float32)
    y = xf * lax.rsqrt(jnp.mean(xf * xf, axis=-1, keepdims=True) + EPS)
    return (y * g.astype(jnp.float32)).astype(x.dtype)


def layer_norm(x, g, b):
    xf = x.astype(jnp.float32)
    mu = jnp.mean(xf, axis=-1, keepdims=True)
    var = jnp.mean(jnp.square(xf - mu), axis=-1, keepdims=True)
    y = (xf - mu) * lax.rsqrt(var + EPS) * g.astype(jnp.float32) + b.astype(jnp.float32)
    return y.astype(x.dtype)


def masked_softmax(logits, mask):
    p = jax.nn.softmax(jnp.where(mask, logits, NEG_INF), axis=-1)
    return jnp.where(mask, p, 0.0)


def rel_bucket(dist):
    n = jnp.maximum(dist, 0)
    nf = jnp.maximum(n, 1).astype(jnp.float32)
    large = MAX_EXACT + (jnp.log(nf / MAX_EXACT) / math.log(MAX_DISTANCE / MAX_EXACT)
                         * (N_BUCKETS - MAX_EXACT)).astype(jnp.int32)
    return jnp.where(n < MAX_EXACT, n, jnp.minimum(large, N_BUCKETS - 1))


def pool_mixer(u, w, scale):
    S = u.shape[1]
    uf = u.astype(jnp.float32)
    csum = jnp.cumsum(uf, axis=1)
    t1 = jnp.arange(1, S + 1, dtype=jnp.float32)
    outs = []
    for g, win in enumerate(POOL_WINDOWS):
        sl = slice(g * POOL_GROUP, (g + 1) * POOL_GROUP)
        cg = csum[..., sl]
        prev = jnp.pad(cg, ((0, 0), (win, 0), (0, 0)))[:, :S]
        mean = (cg - prev) / jnp.minimum(t1, win)[None, :, None]
        outs.append((mean - uf[..., sl]).astype(u.dtype) @ w[g])
    return jnp.concatenate(outs, axis=-1) * scale


def conv_mixer(u, conv_w, conv_b, ln_g, ln_b, pw):
    a, gate = jnp.split(u, 2, axis=-1)
    h = a * jax.nn.sigmoid(gate)
    h = lax.conv_general_dilated(
        h, conv_w[:, None, :], window_strides=(1,), padding=[(CONV_KSIZE - 1, 0)],
        dimension_numbers=('NWC', 'WIO', 'NWC'), feature_group_count=CONV_WIDTH) + conv_b
    h = jax.nn.silu(layer_norm(h, ln_g, ln_b))
    return h @ pw


def compress(kv, pos, w1, w2):
    B, S = kv.shape[0], kv.shape[1]
    nc = (S - CMP_LEN) // CMP_STRIDE + 1
    idx = jnp.arange(nc)[:, None] * CMP_STRIDE + jnp.arange(CMP_LEN)[None, :]
    blocks = kv[:, idx] + pos[None, None, :, None, :]
    flat = blocks.transpose(0, 1, 3, 2, 4).reshape(B, nc, N_KV, CMP_LEN * HEAD_DIM)
    return jax.nn.gelu(flat @ w1) @ w2


def nsa_mixer(q, k_c, v_c, k_s, v_s, k_w, v_w, gates, rel_bias,
              ck_pos, ck_w1, ck_w2, cv_pos, cv_w1, cv_w2):
    B, S = q.shape[0], q.shape[1]
    q = q.reshape(B, S, N_KV, Q_PER_KV, HEAD_DIM) * HEAD_DIM ** -0.5
    gates = gates.reshape(B, S, N_KV, Q_PER_KV, N_BRANCH)
    kc = compress(k_c, ck_pos, ck_w1, ck_w2)
    vc = compress(v_c, cv_pos, cv_w1, cv_w2)
    nc = kc.shape[1]
    n_sel = S // SEL_BLOCK
    top = min(N_SELECT, n_sel)
    ks_blocks = k_s.reshape(B, n_sel, SEL_BLOCK, N_KV, HEAD_DIM).transpose(0, 3, 1, 2, 4)
    vs_blocks = v_s.reshape(B, n_sel, SEL_BLOCK, N_KV, HEAD_DIM).transpose(0, 3, 1, 2, 4)
    kw_pad = jnp.pad(k_w, ((0, 0), (WINDOW, 0), (0, 0), (0, 0)))
    vw_pad = jnp.pad(v_w, ((0, 0), (WINDOW, 0), (0, 0), (0, 0)))
    c_start = jnp.arange(nc) * CMP_STRIDE
    c_end = c_start + CMP_LEN - 1
    s_idx = jnp.arange(n_sel)
    s_start = s_idx * SEL_BLOCK
    overlap = ((c_start[:, None] < s_start[None, :] + SEL_BLOCK)
               & (c_start[:, None] + CMP_LEN > s_start[None, :])).astype(jnp.float32)
    bias_tab = rel_bias.astype(jnp.float32).reshape(N_BUCKETS, N_KV, Q_PER_KV)
    b_ix = jnp.arange(B)[:, None, None, None]
    g_ix = jnp.arange(N_KV)[None, None, :, None]

    def block(i):
        s0 = i * Q_BLOCK
        t = s0 + jnp.arange(Q_BLOCK)
        qb = lax.dynamic_slice_in_dim(q, s0, Q_BLOCK, axis=1)
        gb = lax.dynamic_slice_in_dim(gates, s0, Q_BLOCK, axis=1)
        dist_c = t[:, None] - c_end[None, :]
        lc = (jnp.einsum('bqgrd,bcgd->bqgrc', qb, kc).astype(jnp.float32)
              + bias_tab[rel_bucket(dist_c)].transpose(0, 2, 3, 1)[None])
        pc = masked_softmax(lc, (dist_c >= 0)[:, None, None, :])
        oc = jnp.einsum('bqgrc,bcgd->bqgrd', pc.astype(vc.dtype), vc)
        imp = jnp.einsum('bqgrc,cj->bqgj', pc, overlap)
        cur = t // SEL_BLOCK
        forced = ((s_idx[None, :] == 0) | (s_idx[None, :] == cur[:, None])
                  | (s_idx[None, :] == cur[:, None] - 1))
        causal_s = s_start[None, :] <= t[:, None]
        score = jnp.where(forced[None, :, None, :], FORCE,
                          jnp.where(causal_s[None, :, None, :], imp, NEG_INF))
        _, idx = lax.top_k(score, top)
        ks_g = ks_blocks[b_ix, g_ix, idx].reshape(B, Q_BLOCK, N_KV, top * SEL_BLOCK, HEAD_DIM)
        vs_g = vs_blocks[b_ix, g_ix, idx].reshape(B, Q_BLOCK, N_KV, top * SEL_BLOCK, HEAD_DIM)
        pos_s = (idx[..., None] * SEL_BLOCK + jnp.arange(SEL_BLOCK)).reshape(
            B, Q_BLOCK, N_KV, top * SEL_BLOCK)
        dist_s = t[None, :, None, None] - pos_s
        bias_s = bias_tab[rel_bucket(dist_s), g_ix].transpose(0, 1, 2, 4, 3)
        ls = jnp.einsum('bqgrd,bqgld->bqgrl', qb, ks_g).astype(jnp.float32) + bias_s
        ps = masked_softmax(ls, (dist_s >= 0)[:, :, :, None, :])
        osl = jnp.einsum('bqgrl,bqgld->bqgrd', ps.astype(vs_g.dtype), vs_g)
        kwb = lax.dynamic_slice_in_dim(kw_pad, s0, WINDOW + Q_BLOCK, axis=1)
        vwb = lax.dynamic_slice_in_dim(vw_pad, s0, WINDOW + Q_BLOCK, axis=1)
        pos_w = s0 - WINDOW + jnp.arange(WINDOW + Q_BLOCK)
        dist_w = t[:, None] - pos_w[None, :]
        mask_w = (dist_w >= 0) & (dist_w < WINDOW) & (pos_w[None, :] >= 0)
        lw = (jnp.einsum('bqgrd,bkgd->bqgrk', qb, kwb).astype(jnp.float32)
              + bias_tab[rel_bucket(dist_w)].transpose(0, 2, 3, 1)[None])
        pw = masked_softmax(lw, mask_w[:, None, None, :])
        ow = jnp.einsum('bqgrk,bkgd->bqgrd', pw.astype(vwb.dtype), vwb)
        return gb[..., 0:1] * oc + gb[..., 1:2] * osl + gb[..., 2:3] * ow

    out = lax.map(block, jnp.arange(S // Q_BLOCK))
    return out.transpose(1, 0, 2, 3, 4, 5).reshape(B, S, NSA_WIDTH)


def setup_inputs(seed: int = 0) -> dict:
    key = jax.random.key(seed)
    ks = jax.random.split(key, 40)
    f32 = jnp.float32
    L = DEPTH

    def nrm(k, shape, fan_in):
        return jax.random.normal(k, shape, f32) * fan_in ** -0.5

    def gain(k, shape):
        return 1.0 + 0.05 * jax.random.normal(k, shape, f32)

    def small(k, shape):
        return 0.02 * jax.random.normal(k, shape, f32)

    return {
        'x': jax.random.normal(ks[0], (BATCH, SEQ, D_MODEL), f32),
        'mem': jax.random.normal(ks[1], (BATCH, N_MEM, D_MODEL), f32),
        'rel_bias': 0.5 * jax.random.normal(ks[2], (N_BUCKETS, N_HEADS), f32),
        'mix_pre_g': gain(ks[3], (L, D_MODEL)),
        'mix_post_g': gain(ks[4], (L, D_MODEL)),
        'w_in': nrm(ks[5], (L, D_MODEL, D_IN), D_MODEL),
        'pool_w': nrm(ks[6], (L, len(POOL_WINDOWS), POOL_GROUP, POOL_GROUP), POOL_GROUP),
        'pool_scale': gain(ks[7], (L, POOL_WIDTH)),
        'conv_w': nrm(ks[8], (L, CONV_KSIZE, CONV_WIDTH), CONV_KSIZE),
        'conv_b': small(ks[9], (L, CONV_WIDTH)),
        'conv_ln_g': gain(ks[10], (L, CONV_WIDTH)),
        'conv_ln_b': small(ks[11], (L, CONV_WIDTH)),
        'conv_pw': nrm(ks[12], (L, CONV_WIDTH, CONV_WIDTH), CONV_WIDTH),
        'cmp_k_pos': small(ks[13], (L, CMP_LEN, HEAD_DIM)),
        'cmp_k_w1': nrm(ks[14], (L, CMP_LEN * HEAD_DIM, CMP_HIDDEN), CMP_LEN * HEAD_DIM),
        'cmp_k_w2': nrm(ks[15], (L, CMP_HIDDEN, HEAD_DIM), CMP_HIDDEN),
        'cmp_v_pos': small(ks[16], (L, CMP_LEN, HEAD_DIM)),
        'cmp_v_w1': nrm(ks[17], (L, CMP_LEN * HEAD_DIM, CMP_HIDDEN), CMP_LEN * HEAD_DIM),
        'cmp_v_w2': nrm(ks[18], (L, CMP_HIDDEN, HEAD_DIM), CMP_HIDDEN),
        'w_out': nrm(ks[19], (L, D_MIX, D_MODEL), D_MIX),
        'xa_pre_g': gain(ks[20], (L, D_MODEL)),
        'xa_post_g': gain(ks[21], (L, D_MODEL)),
        'mem_g': gain(ks[22], (L, D_MODEL)),
        'xa_wq': nrm(ks[23], (L, D_MODEL, D_MODEL), D_MODEL),
        'xa_wk': nrm(ks[24], (L, D_MODEL, D_MODEL), D_MODEL),
        'xa_wv': nrm(ks[25], (L, D_MODEL, D_MODEL), D_MODEL),
        'xa_wo': nrm(ks[26], (L, D_MODEL, D_MODEL), D_MODEL),
        'mlp_pre_g': gain(ks[27], (L, D_MODEL)),
        'mlp_post_g': gain(ks[28], (L, D_MODEL)),
        'mlp_w1': nrm(ks[29], (L, D_MODEL, D_FF), D_MODEL),
        'mlp_w2': nrm(ks[30], (L, D_FF, D_MODEL), D_FF),
    }


def reference(x, mem, rel_bias, mix_pre_g, mix_post_g, w_in, pool_w, pool_scale,
              conv_w, conv_b, conv_ln_g, conv_ln_b, conv_pw,
              cmp_k_pos, cmp_k_w1, cmp_k_w2, cmp_v_pos, cmp_v_w1, cmp_v_w2, w_out,
              xa_pre_g, xa_post_g, mem_g, xa_wq, xa_wk, xa_wv, xa_wo,
              mlp_pre_g, mlp_post_g, mlp_w1, mlp_w2):
    B, S = x.shape[0], x.shape[1]
    M = mem.shape[1]
    offsets = [int(o) for o in np.cumsum(IN_SPLITS)[:-1]]
    for l in range(DEPTH):
        h = rms_norm(x, mix_pre_g[l])
        proj = h @ w_in[l]
        u_pool, u_conv, q, kv, gate_logits = jnp.split(proj, offsets, axis=-1)
        k_c, v_c, k_s, v_s, k_w, v_w = [a.reshape(B, S, N_KV, HEAD_DIM)
                                        for a in jnp.split(kv, 6, axis=-1)]
        y_pool = pool_mixer(u_pool, pool_w[l], pool_scale[l])
        y_conv = conv_mixer(u_conv, conv_w[l], conv_b[l], conv_ln_g[l],
                            conv_ln_b[l], conv_pw[l])
        y_nsa = nsa_mixer(q, k_c, v_c, k_s, v_s, k_w, v_w, jax.nn.sigmoid(gate_logits), rel_bias,
                          cmp_k_pos[l], cmp_k_w1[l], cmp_k_w2[l],
                          cmp_v_pos[l], cmp_v_w1[l], cmp_v_w2[l])
        mix = jnp.concatenate([y_pool, y_conv, y_nsa], axis=-1) @ w_out[l]
        x = x + rms_norm(mix, mix_post_g[l])
        h = rms_norm(x, xa_pre_g[l])
        m = rms_norm(mem, mem_g[l])
        qx = (h @ xa_wq[l]).reshape(B, S, XA_HEADS, XA_HEAD_DIM) * XA_HEAD_DIM ** -0.5
        km = (m @ xa_wk[l]).reshape(B, M, XA_HEADS, XA_HEAD_DIM)
        vm = (m @ xa_wv[l]).reshape(B, M, XA_HEADS, XA_HEAD_DIM)
        p = jax.nn.softmax(jnp.einsum('bshd,bmhd->bhsm', qx, km).astype(jnp.float32), axis=-1)
        o = jnp.einsum('bhsm,bmhd->bshd', p.astype(vm.dtype), vm).reshape(B, S, D_MODEL)
        x = x + rms_norm(o @ xa_wo[l], xa_post_g[l])
        h = rms_norm(x, mlp_pre_g[l])
        y = jnp.square(jax.nn.relu(h @ mlp_w1[l])) @ mlp_w2[l]
        x = x + rms_norm(y, mlp_post_g[l])
    return x
```

```python
import functools
import math

import numpy as np
import jax
import jax.numpy as jnp
from jax import lax
from jax.experimental import pallas as pl
from jax.experimental.pallas import tpu as pltpu

F32 = jnp.float32
BF16 = jnp.bfloat16

D_MODEL = 1024
DEPTH = 2
POOL_WIDTH = 256
POOL_WINDOWS = (2, 4, 8, 16)
POOL_GROUP = 64
CONV_WIDTH = 256
CONV_KSIZE = 31
NSA_WIDTH = 512
HEAD_DIM = 64
N_HEADS = 8
N_KV = 2
Q_PER_KV = 4
CMP_LEN = 32
CMP_STRIDE = 16
CMP_HIDDEN = 256
SEL_BLOCK = 64
N_SELECT = 16
WINDOW = 512
N_BRANCH = 3
N_BUCKETS = 32
MAX_EXACT = 16
MAX_DISTANCE = 128
XA_HEADS = 4
XA_HEAD_DIM = 256
D_FF = 4096
EPS = 1e-6
NEG_INF = -1e30
FORCE = 1e30

LANES = 128
ROW_TILE = 512
QBLK = 256
KTILE = 256
QROWS = Q_PER_KV * QBLK
HALO = 32
CNEAR = 24
CNEAR_LO = 8
NEG_BF16 = -(2.0 ** 100)
M_INIT = -1e29
VMEM_LIMIT = 56 * 1024 * 1024


def _rms(x, g):
    return x * lax.rsqrt(jnp.mean(x * x, axis=-1, keepdims=True) + EPS) * g


def _dot(a, b):
    return jnp.dot(a, b, preferred_element_type=F32)


def _dot_nt(a, b):
    return lax.dot_general(a, b, (((1,), (1,)), ((), ())), preferred_element_type=F32)


def _params(*sem):
    return pltpu.CompilerParams(dimension_semantics=sem, vmem_limit_bytes=VMEM_LIMIT)


def _in_proj_kernel(x_ref, g_ref, w_ref, pool_ref, conv_ref, q_ref, kvc_ref, kvs_ref, gate_ref):
    h = _rms(x_ref[...], g_ref[...]).astype(BF16)
    pool_ref[...] = _dot(h, w_ref[:, 0:256])
    conv_ref[...] = _dot(h, w_ref[:, 256:768])
    q_ref[...] = (_dot(h, w_ref[:, 768:1280]) * (HEAD_DIM ** -0.5)).astype(BF16)
    kvc_ref[...] = _dot(h, w_ref[:, 1280:1536])
    kvs_ref[...] = _dot(h, w_ref[:, 1536:2048]).astype(BF16)
    gate_ref[...] = jax.nn.sigmoid(_dot(h, w_ref[:, 2048:2176]))


def _in_proj(xf, g, w):
    T = xf.shape[0]
    row = lambda n: pl.BlockSpec((ROW_TILE, n), lambda i: (i, 0))
    full = lambda a: pl.BlockSpec(a.shape, lambda i: (0,) * a.ndim)
    return pl.pallas_call(
        _in_proj_kernel,
        grid=(T // ROW_TILE,),
        in_specs=[row(D_MODEL), full(g), full(w)],
        out_specs=[row(256), row(512), row(512), row(256), row(512), row(128)],
        out_shape=[
            jax.ShapeDtypeStruct((T, 256), F32),
            jax.ShapeDtypeStruct((T, 512), F32),
            jax.ShapeDtypeStruct((T, 512), BF16),
            jax.ShapeDtypeStruct((T, 256), F32),
            jax.ShapeDtypeStruct((T, 512), BF16),
            jax.ShapeDtypeStruct((T, 128), F32),
        ],
        compiler_params=_params("parallel"),
        name="in_proj",
    )(xf, g, w)


def _mixers_kernel(up_ref, uc_ref, pw_ref, ps_ref, cw_ref, cb_ref, lg_ref, lb_ref, cpw_ref,
                   yp_ref, yc_ref, ubuf, hbuf):
    s = pl.program_id(1)
    ts = up_ref.shape[1]

    @pl.when(s == 0)
    def _():
        ubuf[0:HALO, :] = jnp.zeros((HALO, POOL_WIDTH), F32)
        hbuf[0:HALO, :] = jnp.zeros((HALO, CONV_WIDTH), F32)

    u = up_ref[0]
    ubuf[HALO:HALO + ts, :] = u
    back = lambda k, c0, c1: ubuf[HALO - k:HALO - k + ts, c0:c1]
    lo = u[:, 0:128] + back(1, 0, 128)
    s4 = lo + back(2, 0, 128) + back(3, 0, 128)
    hi = u[:, 128:256]
    for k in range(1, 8):
        hi = hi + back(k, 128, 256)
    s16 = hi
    for k in range(8, 16):
        s16 = s16 + back(k, 128, 256)
    t1 = (s * ts + lax.broadcasted_iota(jnp.int32, (ts, 128), 0) + 1).astype(F32)
    lane = lax.broadcasted_iota(jnp.int32, (ts, 128), 1)
    first = lane < POOL_GROUP
    sum_lo = jnp.where(first, lo, s4)
    den_lo = jnp.minimum(t1, jnp.where(first, 2.0, 4.0))
    sum_hi = jnp.where(first, hi, s16)
    den_hi = jnp.minimum(t1, jnp.where(first, 8.0, 16.0))
    d = jnp.concatenate([sum_lo / den_lo - u[:, 0:128], sum_hi / den_hi - u[:, 128:256]], axis=1)
    yp_ref[0] = (_dot(d.astype(BF16), pw_ref[...]) * ps_ref[...]).astype(BF16)
    ubuf[0:HALO, :] = ubuf[ts:ts + HALO, :]

    uc = uc_ref[0]
    hbuf[HALO:HALO + ts, :] = uc[:, 0:CONV_WIDTH] * jax.nn.sigmoid(uc[:, CONV_WIDTH:])
    acc = jnp.zeros((ts, CONV_WIDTH), F32) + cb_ref[...]
    for j in range(CONV_KSIZE):
        o = HALO - (CONV_KSIZE - 1) + j
        acc = acc + hbuf[o:o + ts, :] * cw_ref[j:j + 1, :]
    mu = jnp.mean(acc, axis=-1, keepdims=True)
    var = jnp.mean(jnp.square(acc - mu), axis=-1, keepdims=True)
    y = (acc - mu) * lax.rsqrt(var + EPS) * lg_ref[...] + lb_ref[...]
    y = y * jax.nn.sigmoid(y)
    yc_ref[0] = _dot(y.astype(BF16), cpw_ref[...]).astype(BF16)
    hbuf[0:HALO, :] = hbuf[ts:ts + HALO, :]


def _mixers(up, uc, pw, ps, cw, cb, lg, lb, cpw):
    B, S, _ = up.shape
    ts = ROW_TILE
    seq = lambda n: pl.BlockSpec((1, ts, n), lambda b, s: (b, s, 0))
    full = lambda a: pl.BlockSpec(a.shape, lambda b, s: (0,) * a.ndim)
    return pl.pallas_call(
        _mixers_kernel,
        grid=(B, S // ts),
        in_specs=[seq(256), seq(512)] + [full(a) for a in (pw, ps, cw, cb, lg, lb, cpw)],
        out_specs=[seq(256), seq(256)],
        out_shape=[jax.ShapeDtypeStruct((B, S, 256), BF16)] * 2,
        scratch_shapes=[pltpu.VMEM((HALO + ts, POOL_WIDTH), F32), pltpu.VMEM((HALO + ts, CONV_WIDTH), F32)],
        compiler_params=_params("parallel", "arbitrary"),
        name="mixers",
    )(up, uc, pw, ps, cw, cb, lg, lb, cpw)


def _compress_kernel(r_ref, pos_ref, w1_ref, w2_ref, o_ref):
    r = r_ref[0, 0, 0]
    ncp = r.shape[0]
    half = CMP_STRIDE * HEAD_DIM
    a = _dot((r + pos_ref[0, 0]).astype(BF16), w1_ref[0, 0:half, :])
    b = _dot((r + pos_ref[0, 1]).astype(BF16), w1_ref[0, half:2 * half, :])
    pre = a + pltpu.roll(b, ncp - 1, 0)
    o_ref[0, 0, 0] = _dot(jax.nn.gelu(pre).astype(BF16), w2_ref[0]).astype(BF16)


def _compress(r, pos, w1, w2):
    B, _, G, ncp, _ = r.shape
    return pl.pallas_call(
        _compress_kernel,
        grid=(B, 2, G),
        in_specs=[
            pl.BlockSpec((1, 1, 1, ncp, 1024), lambda b, k, g: (b, k, g, 0, 0)),
            pl.BlockSpec((1, 2, 1, 1024), lambda b, k, g: (k, 0, 0, 0)),
            pl.BlockSpec((1, 2048, CMP_HIDDEN), lambda b, k, g: (k, 0, 0)),
            pl.BlockSpec((1, CMP_HIDDEN, HEAD_DIM), lambda b, k, g: (k, 0, 0)),
        ],
        out_specs=pl.BlockSpec((1, 1, 1, ncp, HEAD_DIM), lambda b, k, g: (b, k, g, 0, 0)),
        out_shape=jax.ShapeDtypeStruct((B, 2, G, ncp, HEAD_DIM), BF16),
        compiler_params=_params("parallel", "parallel", "parallel"),
        name="compress",
    )(r, pos, w1, w2)


def _rel_bucket_np(dist):
    n = np.maximum(dist, 0)
    nf = np.maximum(n, 1).astype(np.float64)
    large = MAX_EXACT + (np.log(nf / MAX_EXACT) / math.log(MAX_DISTANCE / MAX_EXACT)
                         * (N_BUCKETS - MAX_EXACT)).astype(np.int64)
    return np.where(n < MAX_EXACT, n, np.minimum(large, N_BUCKETS - 1)).astype(np.int32)


def _static_maps():
    qq = np.arange(QBLK)[:, None]
    kk = np.arange(WINDOW + QBLK)[None, :]
    dist = qq + WINDOW - kk
    wmap = np.where((dist >= 0) & (dist < WINDOW), _rel_bucket_np(dist), -1)
    u = np.arange(LANES)[None, :]
    dist_c = qq - CMP_STRIDE * (u - CNEAR_LO) - (CMP_LEN - 1)
    cmap = np.where((dist_c >= 0) & (u < CNEAR), _rel_bucket_np(dist_c), -1)
    cmap = np.where(u < CNEAR, cmap, -2)
    return wmap.astype(np.int32), cmap.astype(np.int32)


def _tables_kernel(bias_ref, wmap_ref, cmap_ref, tw_ref, tc_ref):
    h = pl.program_id(0)
    far = bias_ref[N_BUCKETS - 1, h]

    def lookup(m):
        out = jnp.where(m == -1, NEG_INF, 0.0).astype(F32)
        for b in range(N_BUCKETS):
            out = jnp.where(m == b, bias_ref[b, h] - far, out)
        return out

    tw_ref[0] = lookup(wmap_ref[...])
    cmap = cmap_ref[...]
    val = lookup(cmap)
    hi = val.astype(BF16)
    lo = (val - hi.astype(F32)).astype(BF16)
    col = lax.broadcasted_iota(jnp.int32, cmap.shape, 1)
    hi_part = jnp.where(col < CNEAR, hi.astype(F32), 0.0)
    lo_part = pltpu.roll(jnp.where(col < CNEAR, lo.astype(F32), 0.0), CNEAR, 1)
    one = jnp.where(col == 2 * CNEAR, 1.0, 0.0)
    tc_ref[0] = (hi_part + lo_part + one).astype(BF16)


def _bias_tables(rel_bias):
    wmap, cmap = _static_maps()
    nw = WINDOW + QBLK
    return pl.pallas_call(
        _tables_kernel,
        grid=(N_HEADS,),
        in_specs=[
            pl.BlockSpec(memory_space=pltpu.SMEM),
            pl.BlockSpec((QBLK, nw), lambda h: (0, 0)),
            pl.BlockSpec((QBLK, LANES), lambda h: (0, 0)),
        ],
        out_specs=[
            pl.BlockSpec((1, QBLK, nw), lambda h: (h, 0, 0)),
            pl.BlockSpec((1, QBLK, LANES), lambda h: (h, 0, 0)),
        ],
        out_shape=[
            jax.ShapeDtypeStruct((N_HEADS, QBLK, nw), F32),
            jax.ShapeDtypeStruct((N_HEADS, QBLK, LANES), BF16),
        ],
        compiler_params=_params("parallel"),
        name="bias_tables",
    )(rel_bias, jnp.asarray(wmap), jnp.asarray(cmap))


def _nsa_kernel(q_ref, kc_ref, vc_ref, ksa_ref, vsa_ref, kwp_ref, vwa_ref, gate_ref, tw_ref, tc_ref,
                ovl_ref, o_ref, m_ref, acc_ref):
    i = pl.program_id(2)
    q = q_ref[0, 0, 0]
    ncp = kc_ref.shape[2]
    nb = ovl_ref.shape[1]

    s = _dot_nt(q, kc_ref[0, 0])
    u = lax.broadcasted_iota(jnp.int32, (LANES, ncp), 0)
    c = lax.broadcasted_iota(jnp.int32, (LANES, ncp), 1)
    rel = c - (CMP_STRIDE * i - CNEAR_LO)
    place = jnp.where((rel == u) & (u < CNEAR), 1.0, 0.0)
    place = jnp.where((rel == u - CNEAR) & (u >= CNEAR) & (u < 2 * CNEAR), 1.0, place)
    place = jnp.where((u == 2 * CNEAR) & (rel >= CNEAR), NEG_BF16, place)
    s = s + _dot(tc_ref[0], place.astype(BF16))
    m = jnp.maximum(jnp.max(s, axis=1, keepdims=True), M_INIT)
    p = jnp.exp(s - m)
    l = jnp.sum(p, axis=1, keepdims=True)
    pn = p * (1.0 / jnp.maximum(l, 1e-30))
    o_c = _dot(pn.astype(BF16), vc_ref[0, 0])

    psum = pn[0:QBLK] + pn[QBLK:2 * QBLK] + pn[2 * QBLK:3 * QBLK] + pn[3 * QBLK:4 * QBLK]
    p_hi = psum.astype(BF16)
    p_lo = (psum - p_hi.astype(F32)).astype(BF16)
    imp = _dot(p_hi, ovl_ref[...]) + _dot(p_lo, ovl_ref[...])
    qq = lax.broadcasted_iota(jnp.int32, (QBLK, nb), 0)
    jb = lax.broadcasted_iota(jnp.int32, (QBLK, nb), 1)
    cur = (QBLK // SEL_BLOCK) * i + (qq >> 6)
    forced = (jb == 0) | (jb == cur) | (jb == cur - 1)
    score = jnp.where(forced, FORCE, jnp.where(jb <= cur, imp, NEG_INF))
    sc = score.T
    jio = lax.broadcasted_iota(jnp.int32, (nb, QBLK), 0)
    msel = jnp.full((nb, QBLK), NEG_BF16, F32)
    for _ in range(N_SELECT):
        mx = jnp.max(sc, axis=0, keepdims=True)
        idx = jnp.min(jnp.where(sc == mx, jio, nb), axis=0, keepdims=True)
        hit = jio == idx
        msel = jnp.where(hit, 0.0, msel)
        sc = jnp.where(hit, -3e38, sc)
    mrow = msel.T.astype(BF16)
    qaug = jnp.concatenate([jnp.concatenate([mrow] * Q_PER_KV, axis=0), q], axis=1)

    def reset():
        m_ref[...] = jnp.full(m_ref.shape, M_INIT, F32)
        acc_ref[...] = jnp.zeros(acc_ref.shape, F32)

    def update(sl, vt):
        m_prev = m_ref[...]
        m_next = jnp.maximum(m_prev, jnp.max(sl, axis=1)[:, None])
        alpha = jnp.exp(m_prev - m_next)
        pt = jnp.exp(sl - jnp.concatenate([m_next] * (KTILE // LANES), axis=1))
        acc_ref[...] = alpha * acc_ref[...] + _dot(pt.astype(BF16), vt)
        m_ref[...] = m_next

    def finish():
        acc = acc_ref[...]
        return acc[:, 0:HEAD_DIM], acc[:, HEAD_DIM:HEAD_DIM + 1]

    def sel_tile(jj, table):
        off = pl.multiple_of(jj * KTILE, KTILE)
        sl = _dot_nt(qaug, ksa_ref[0, 0, pl.ds(off, KTILE), :])
        if table is not None:
            sl = sl + table
        update(sl, vsa_ref[0, 0, pl.ds(off, KTILE), :])

    reset()

    def far_body(jj, carry):
        sel_tile(jj, None)
        return carry

    lax.fori_loop(0, jnp.maximum(i - 1, 0), far_body, 0)

    @pl.when(i >= 1)
    def _():
        sel_tile(i - 1, tw_ref[0, :, WINDOW - QBLK:WINDOW])

    sel_tile(i, tw_ref[0, :, WINDOW:WINDOW + QBLK])
    o_s, l_s = finish()

    reset()
    for tt in range(WINDOW // KTILE + 1):
        def win_tile(tt=tt):
            off = pl.multiple_of((i + tt) * KTILE, KTILE)
            sl = _dot_nt(q, kwp_ref[0, 0, pl.ds(off, KTILE), :]) + tw_ref[0, :, tt * KTILE:(tt + 1) * KTILE]
            update(sl, vwa_ref[0, 0, pl.ds(off, KTILE), :])

        if tt == WINDOW // KTILE:
            win_tile()
        else:
            pl.when(i + tt >= WINDOW // KTILE)(win_tile)
    o_w, l_w = finish()

    g = gate_ref[0, 0, 0]
    out = (g[:, 0:1] * o_c[:, 0:HEAD_DIM]
           + (g[:, 1:2] / jnp.maximum(l_s, 1e-30)) * o_s
           + (g[:, 2:3] / jnp.maximum(l_w, 1e-30)) * o_w)
    o_ref[0, 0, 0] = out.astype(BF16)


def _nsa(qa, kc, vc, ksa, vsa, kwp, vwa, gates, tw, tc, ovl):
    B, G, NQ, _, _ = qa.shape
    S = ksa.shape[2]
    ncp = kc.shape[2]
    nb = ovl.shape[1]
    per_bg = lambda a: pl.BlockSpec((1, 1) + a.shape[2:], lambda b, g, i: (b, g) + (0,) * (a.ndim - 2))
    per_g = lambda a: pl.BlockSpec((1,) + a.shape[1:], lambda b, g, i: (g,) + (0,) * (a.ndim - 1))
    blk = lambda n: pl.BlockSpec((1, 1, 1, QROWS, n), lambda b, g, i: (b, g, i, 0, 0))
    return pl.pallas_call(
        _nsa_kernel,
        grid=(B, G, NQ),
        in_specs=[blk(LANES), per_bg(kc), per_bg(vc), per_bg(ksa), per_bg(vsa), per_bg(kwp), per_bg(vwa),
                  blk(N_BRANCH), per_g(tw), per_g(tc), pl.BlockSpec(ovl.shape, lambda b, g, i: (0, 0))],
        out_specs=blk(HEAD_DIM),
        out_shape=jax.ShapeDtypeStruct((B, G, NQ, QROWS, HEAD_DIM), BF16),
        scratch_shapes=[pltpu.VMEM((QROWS, LANES), F32), pltpu.VMEM((QROWS, LANES), F32)],
        compiler_params=_params("parallel", "parallel", "arbitrary"),
        name="nsa",
    )(qa, kc, vc, ksa, vsa, kwp, vwa, gates, tw, tc, ovl)


def _out_proj_kernel(x_ref, yp_ref, yc_ref, yn_ref, w_ref, g_ref, o_ref):
    mix = (_dot(yp_ref[...], w_ref[0:256, :]) + _dot(yc_ref[...], w_ref[256:512, :])
           + _dot(yn_ref[...], w_ref[512:1024, :]))
    o_ref[...] = x_ref[...] + _rms(mix, g_ref[...])


def _out_proj(xf, yp, yc, yn, w, g):
    T = xf.shape[0]
    row = lambda n: pl.BlockSpec((ROW_TILE, n), lambda i: (i, 0))
    full = lambda a: pl.BlockSpec(a.shape, lambda i: (0,) * a.ndim)
    return pl.pallas_call(
        _out_proj_kernel,
        grid=(T // ROW_TILE,),
        in_specs=[row(D_MODEL), row(256), row(256), row(512), full(w), full(g)],
        out_specs=row(D_MODEL),
        out_shape=jax.ShapeDtypeStruct((T, D_MODEL), F32),
        compiler_params=_params("parallel"),
        name="out_proj",
    )(xf, yp, yc, yn, w, g)


def _mem_kv_kernel(mem_ref, g_ref, wk_ref, wv_ref, k_ref, v_ref):
    m = _rms(mem_ref[0], g_ref[...]).astype(BF16)
    k_ref[0] = _dot(m, wk_ref[...]).astype(BF16)
    v_ref[0] = _dot(m, wv_ref[...]).astype(BF16)


def _mem_kv(mem, g, wk, wv):
    B, M, _ = mem.shape
    full = lambda a: pl.BlockSpec(a.shape, lambda b: (0,) * a.ndim)
    per_b = pl.BlockSpec((1, M, D_MODEL), lambda b: (b, 0, 0))
    return pl.pallas_call(
        _mem_kv_kernel,
        grid=(B,),
        in_specs=[per_b, full(g), full(wk), full(wv)],
        out_specs=[per_b, per_b],
        out_shape=[jax.ShapeDtypeStruct((B, M, D_MODEL), BF16)] * 2,
        compiler_params=_params("parallel"),
        name="mem_kv",
    )(mem, g, wk, wv)


def _xattn_kernel(x_ref, gpre_ref, wq_ref, k_ref, v_ref, wo_ref, gpost_ref, o_ref):
    x = x_ref[0]
    h = _rms(x, gpre_ref[...]).astype(BF16)
    qx = (_dot(h, wq_ref[...]) * (XA_HEAD_DIM ** -0.5)).astype(BF16)
    outs = []
    for hh in range(XA_HEADS):
        sl = slice(hh * XA_HEAD_DIM, (hh + 1) * XA_HEAD_DIM)
        s = _dot_nt(qx[:, sl], k_ref[0, :, sl])
        p = jnp.exp(s - jnp.max(s, axis=1, keepdims=True))
        p = p * (1.0 / jnp.sum(p, axis=1, keepdims=True))
        outs.append(_dot(p.astype(BF16), v_ref[0, :, sl]).astype(BF16))
    o = jnp.concatenate(outs, axis=1)
    o_ref[0] = x + _rms(_dot(o, wo_ref[...]), gpost_ref[...])


def _xattn(x, gpre, wq, km, vm, wo, gpost):
    B, S, _ = x.shape
    M = km.shape[1]
    ts = ROW_TILE
    full = lambda a: pl.BlockSpec(a.shape, lambda b, s: (0,) * a.ndim)
    seq = pl.BlockSpec((1, ts, D_MODEL), lambda b, s: (b, s, 0))
    per_b = pl.BlockSpec((1, M, D_MODEL), lambda b, s: (b, 0, 0))
    return pl.pallas_call(
        _xattn_kernel,
        grid=(B, S // ts),
        in_specs=[seq, full(gpre), full(wq), per_b, per_b, full(wo), full(gpost)],
        out_specs=seq,
        out_shape=jax.ShapeDtypeStruct((B, S, D_MODEL), F32),
        compiler_params=_params("parallel", "parallel"),
        name="xattn",
    )(x, gpre, wq, km, vm, wo, gpost)


def _mlp_kernel(x_ref, gpre_ref, w1_ref, w2_ref, gpost_ref, o_ref):
    x = x_ref[...]
    h = _rms(x, gpre_ref[...]).astype(BF16)
    y = jnp.zeros(x.shape, F32)
    chunk = 1024
    for c in range(D_FF // chunk):
        a = jnp.maximum(_dot(h, w1_ref[:, c * chunk:(c + 1) * chunk]), 0.0)
        y = y + _dot((a * a).astype(BF16), w2_ref[c * chunk:(c + 1) * chunk, :])
    o_ref[...] = x + _rms(y, gpost_ref[...])


def _mlp(xf, gpre, w1, w2, gpost):
    T = xf.shape[0]
    row = pl.BlockSpec((ROW_TILE, D_MODEL), lambda i: (i, 0))
    full = lambda a: pl.BlockSpec(a.shape, lambda i: (0,) * a.ndim)
    return pl.pallas_call(
        _mlp_kernel,
        grid=(T // ROW_TILE,),
        in_specs=[row, full(gpre), full(w1), full(w2), full(gpost)],
        out_specs=row,
        out_shape=jax.ShapeDtypeStruct((T, D_MODEL), F32),
        compiler_params=_params("parallel"),
        name="mlp",
    )(xf, gpre, w1, w2, gpost)


def _heads_major(a, B, S):
    return a.reshape(B, S, N_KV, HEAD_DIM).transpose(0, 2, 1, 3)


def kernel(x, mem, rel_bias, mix_pre_g, mix_post_g, w_in, pool_w, pool_scale, conv_w, conv_b, conv_ln_g,
           conv_ln_b, conv_pw, cmp_k_pos, cmp_k_w1, cmp_k_w2, cmp_v_pos, cmp_v_w1, cmp_v_w2, w_out,
           xa_pre_g, xa_post_g, mem_g, xa_wq, xa_wk, xa_wv, xa_wo, mlp_pre_g, mlp_post_g, mlp_w1, mlp_w2):
    B, S, _ = x.shape
    assert S % ROW_TILE == 0 and S // SEL_BLOCK >= N_SELECT and S // SEL_BLOCK <= LANES
    T = B * S
    NQ = S // QBLK
    ncp = S // CMP_STRIDE
    nb = LANES
    row2 = lambda a: a.reshape(1, -1)

    tw, tc = _bias_tables(rel_bias)
    tw = tw.reshape(N_KV, QROWS, WINDOW + QBLK)
    tc = tc.reshape(N_KV, QROWS, LANES)

    pos = np.arange(S)
    blk_onehot = jnp.asarray((pos[:, None] // SEL_BLOCK == np.arange(nb)[None, :]), BF16)
    cs = np.arange(ncp)[:, None] * CMP_STRIDE
    ss = np.arange(nb)[None, :] * SEL_BLOCK
    ovl = (cs < ss + SEL_BLOCK) & (cs + CMP_LEN > ss) & (np.arange(ncp)[:, None] < ncp - 1) & (ss < S)
    ovl = jnp.asarray(ovl, BF16)

    xf = x.reshape(T, D_MODEL)
    for l in range(DEPTH):
        w_in_l = jnp.pad(w_in[l], ((0, 0), (0, 2176 - w_in.shape[2]))).astype(BF16)
        up, uc, q, kvc, kvs, gates = _in_proj(xf, row2(mix_pre_g[l]), w_in_l)

        pw_bd = jnp.zeros((POOL_WIDTH, POOL_WIDTH), F32)
        for g in range(len(POOL_WINDOWS)):
            pw_bd = lax.dynamic_update_slice(pw_bd, pool_w[l, g], (g * POOL_GROUP, g * POOL_GROUP))
        yp, yc = _mixers(up.reshape(B, S, 256), uc.reshape(B, S, 512), pw_bd.astype(BF16),
                         row2(pool_scale[l]), conv_w[l], row2(conv_b[l]), row2(conv_ln_g[l]),
                         row2(conv_ln_b[l]), conv_pw[l].astype(BF16))

        rk = _heads_major(kvc[:, 0:128], B, S).reshape(B, N_KV, ncp, CMP_STRIDE * HEAD_DIM)
        rv = _heads_major(kvc[:, 128:256], B, S).reshape(B, N_KV, ncp, CMP_STRIDE * HEAD_DIM)
        cpos = jnp.stack([cmp_k_pos[l], cmp_v_pos[l]]).reshape(2, 2, 1, CMP_STRIDE * HEAD_DIM)
        cw1 = jnp.stack([cmp_k_w1[l], cmp_v_w1[l]]).astype(BF16)
        cw2 = jnp.stack([cmp_k_w2[l], cmp_v_w2[l]]).astype(BF16)
        kcv = _compress(jnp.stack([rk, rv], axis=1), cpos, cw1, cw2)
        kcv = jnp.pad(kcv, ((0, 0),) * 4 + ((0, LANES - HEAD_DIM),))

        qa = q.reshape(B, NQ, QBLK, N_KV, Q_PER_KV, HEAD_DIM).transpose(0, 3, 1, 4, 2, 5)
        qa = jnp.pad(qa.reshape(B, N_KV, NQ, QROWS, HEAD_DIM), ((0, 0),) * 4 + ((0, LANES - HEAD_DIM),))
        ga = gates[:, 0:N_HEADS * N_BRANCH].reshape(B, NQ, QBLK, N_KV, Q_PER_KV, N_BRANCH)
        ga = ga.transpose(0, 3, 1, 4, 2, 5).reshape(B, N_KV, NQ, QROWS, N_BRANCH)
        k_s, v_s, k_w, v_w = [_heads_major(kvs[:, j * 128:(j + 1) * 128], B, S) for j in range(4)]
        zeros = jnp.zeros((B, N_KV, S, LANES - HEAD_DIM), BF16)
        ones = jnp.zeros((B, N_KV, S, LANES - HEAD_DIM), BF16).at[..., 0].set(1)
        ksa = jnp.concatenate([jnp.broadcast_to(blk_onehot, (B, N_KV, S, nb)), k_s, zeros], axis=-1)
        vsa = jnp.concatenate([v_s, ones], axis=-1)
        front = ((0, 0), (0, 0), (WINDOW, 0), (0, 0))
        kwp = jnp.pad(jnp.concatenate([k_w, zeros], axis=-1), front)
        vwa = jnp.pad(jnp.concatenate([v_w, ones], axis=-1), front)
        yn = _nsa(qa, kcv[:, 0], kcv[:, 1], ksa, vsa, kwp, vwa, ga, tw, tc, ovl)
        yn = yn.reshape(B, N_KV, NQ, Q_PER_KV, QBLK, HEAD_DIM).transpose(0, 2, 4, 1, 3, 5).reshape(T, NSA_WIDTH)

        xf = _out_proj(xf, yp.reshape(T, 256), yc.reshape(T, 256), yn, w_out[l].astype(BF16),
                       row2(mix_post_g[l]))

        km, vm = _mem_kv(mem, row2(mem_g[l]), xa_wk[l].astype(BF16), xa_wv[l].astype(BF16))
        xf = _xattn(xf.reshape(B, S, D_MODEL), row2(xa_pre_g[l]), xa_wq[l].astype(BF16), km, vm,
                    xa_wo[l].astype(BF16), row2(xa_post_g[l])).reshape(T, D_MODEL)

        xf = _mlp(xf, row2(mlp_pre_g[l]), mlp_w1[l].astype(BF16), mlp_w2[l].astype(BF16),
                  row2(mlp_post_g[l]))
    return xf.reshape(B, S, D_MODEL)
```

```python
import functools
import math

import numpy as np
import jax
import jax.numpy as jnp
from jax import lax
from jax.experimental import pallas as pl
from jax.experimental.pallas import tpu as pltpu

F32 = jnp.float32
BF16 = jnp.bfloat16

D_MODEL = 1024
DEPTH = 2
POOL_WIDTH = 256
POOL_WINDOWS = (2, 4, 8, 16)
POOL_GROUP = 64
CONV_WIDTH = 256
CONV_KSIZE = 31
NSA_WIDTH = 512
HEAD_DIM = 64
N_HEADS = 8
N_KV = 2
Q_PER_KV = 4
CMP_LEN = 32
CMP_STRIDE = 16
CMP_HIDDEN = 256
SEL_BLOCK = 64
N_SELECT = 16
WINDOW = 512
N_BRANCH = 3
N_BUCKETS = 32
MAX_EXACT = 16
MAX_DISTANCE = 128
XA_HEADS = 4
XA_HEAD_DIM = 256
D_FF = 4096
EPS = 1e-6
NEG_INF = -1e30
FORCE = 1e30

LANES = 128
ROW_TILE = 512
QBLK = 256
KTILE = 256
QROWS = Q_PER_KV * QBLK
HALO = 32
CNEAR = 24
CNEAR_LO = 8
NEG_BF16 = -(2.0 ** 100)
M_INIT = -1e29
TAKEN = -3e38
N_FORCED = 3
LOG2E = 1.4426950408889634
FAR_UNROLL = 2
VROWS = 80
VMEM_LIMIT = 56 * 1024 * 1024


def _rms(x, g):
    return x * lax.rsqrt(jnp.mean(x * x, axis=-1, keepdims=True) + EPS) * g


def _dot(a, b):
    return jnp.dot(a, b, preferred_element_type=F32)


def _dot_nt(a, b):
    return lax.dot_general(a, b, (((1,), (1,)), ((), ())), preferred_element_type=F32)


def _params(*sem):
    return pltpu.CompilerParams(dimension_semantics=sem, vmem_limit_bytes=VMEM_LIMIT)


def _in_proj_kernel(x_ref, g_ref, w_ref, pool_ref, conv_ref, q_ref, kvc_ref, kvs_ref, gate_ref):
    h = _rms(x_ref[...], g_ref[...]).astype(BF16)
    pool_ref[...] = _dot(h, w_ref[:, 0:256])
    conv_ref[...] = _dot(h, w_ref[:, 256:768])
    q_ref[...] = (_dot(h, w_ref[:, 768:1280]) * (HEAD_DIM ** -0.5 * LOG2E)).astype(BF16)
    kvc_ref[...] = _dot(h, w_ref[:, 1280:1536])
    kvs_ref[...] = _dot(h, w_ref[:, 1536:2048]).astype(BF16)
    gate_ref[...] = jax.nn.sigmoid(_dot(h, w_ref[:, 2048:2176]))


def _in_proj(xf, g, w):
    T = xf.shape[0]
    row = lambda n: pl.BlockSpec((ROW_TILE, n), lambda i: (i, 0))
    full = lambda a: pl.BlockSpec(a.shape, lambda i: (0,) * a.ndim)
    return pl.pallas_call(
        _in_proj_kernel,
        grid=(T // ROW_TILE,),
        in_specs=[row(D_MODEL), full(g), full(w)],
        out_specs=[row(256), row(512), row(512), row(256), row(512), row(128)],
        out_shape=[
            jax.ShapeDtypeStruct((T, 256), F32),
            jax.ShapeDtypeStruct((T, 512), F32),
            jax.ShapeDtypeStruct((T, 512), BF16),
            jax.ShapeDtypeStruct((T, 256), F32),
            jax.ShapeDtypeStruct((T, 512), BF16),
            jax.ShapeDtypeStruct((T, 128), F32),
        ],
        compiler_params=_params("parallel"),
        name="in_proj",
    )(xf, g, w)


def _mixers_kernel(up_ref, uc_ref, pw_ref, ps_ref, cw_ref, cb_ref, lg_ref, lb_ref, cpw_ref,
                   yp_ref, yc_ref, ubuf, hbuf):
    s = pl.program_id(1)
    ts = up_ref.shape[1]

    @pl.when(s == 0)
    def _():
        ubuf[0:HALO, :] = jnp.zeros((HALO, POOL_WIDTH), F32)
        hbuf[0:HALO, :] = jnp.zeros((HALO, CONV_WIDTH), F32)

    u = up_ref[0]
    ubuf[HALO:HALO + ts, :] = u
    back = lambda k, c0, c1: ubuf[HALO - k:HALO - k + ts, c0:c1]
    lo = u[:, 0:128] + back(1, 0, 128)
    s4 = lo + back(2, 0, 128) + back(3, 0, 128)
    hi = u[:, 128:256]
    for k in range(1, 8):
        hi = hi + back(k, 128, 256)
    s16 = hi
    for k in range(8, 16):
        s16 = s16 + back(k, 128, 256)
    t1 = (s * ts + lax.broadcasted_iota(jnp.int32, (ts, 128), 0) + 1).astype(F32)
    lane = lax.broadcasted_iota(jnp.int32, (ts, 128), 1)
    first = lane < POOL_GROUP
    sum_lo = jnp.where(first, lo, s4)
    den_lo = jnp.minimum(t1, jnp.where(first, 2.0, 4.0))
    sum_hi = jnp.where(first, hi, s16)
    den_hi = jnp.minimum(t1, jnp.where(first, 8.0, 16.0))
    d = jnp.concatenate([sum_lo / den_lo - u[:, 0:128], sum_hi / den_hi - u[:, 128:256]], axis=1)
    yp_ref[0] = (_dot(d.astype(BF16), pw_ref[...]) * ps_ref[...]).astype(BF16)
    ubuf[0:HALO, :] = ubuf[ts:ts + HALO, :]

    uc = uc_ref[0]
    hbuf[HALO:HALO + ts, :] = uc[:, 0:CONV_WIDTH] * jax.nn.sigmoid(uc[:, CONV_WIDTH:])
    acc = jnp.zeros((ts, CONV_WIDTH), F32) + cb_ref[...]
    for j in range(CONV_KSIZE):
        o = HALO - (CONV_KSIZE - 1) + j
        acc = acc + hbuf[o:o + ts, :] * cw_ref[j:j + 1, :]
    mu = jnp.mean(acc, axis=-1, keepdims=True)
    var = jnp.mean(jnp.square(acc - mu), axis=-1, keepdims=True)
    y = (acc - mu) * lax.rsqrt(var + EPS) * lg_ref[...] + lb_ref[...]
    y = y * jax.nn.sigmoid(y)
    yc_ref[0] = _dot(y.astype(BF16), cpw_ref[...]).astype(BF16)
    hbuf[0:HALO, :] = hbuf[ts:ts + HALO, :]


def _mixers(up, uc, pw, ps, cw, cb, lg, lb, cpw):
    B, S, _ = up.shape
    ts = ROW_TILE
    seq = lambda n: pl.BlockSpec((1, ts, n), lambda b, s: (b, s, 0))
    full = lambda a: pl.BlockSpec(a.shape, lambda b, s: (0,) * a.ndim)
    return pl.pallas_call(
        _mixers_kernel,
        grid=(B, S // ts),
        in_specs=[seq(256), seq(512)] + [full(a) for a in (pw, ps, cw, cb, lg, lb, cpw)],
        out_specs=[seq(256), seq(256)],
        out_shape=[jax.ShapeDtypeStruct((B, S, 256), BF16)] * 2,
        scratch_shapes=[pltpu.VMEM((HALO + ts, POOL_WIDTH), F32), pltpu.VMEM((HALO + ts, CONV_WIDTH), F32)],
        compiler_params=_params("parallel", "arbitrary"),
        name="mixers",
    )(up, uc, pw, ps, cw, cb, lg, lb, cpw)


def _compress_kernel(r_ref, pos_ref, w1_ref, w2_ref, o_ref):
    r = r_ref[0, 0, 0]
    ncp = r.shape[0]
    half = CMP_STRIDE * HEAD_DIM
    a = _dot((r + pos_ref[0, 0]).astype(BF16), w1_ref[0, 0:half, :])
    b = _dot((r + pos_ref[0, 1]).astype(BF16), w1_ref[0, half:2 * half, :])
    pre = a + pltpu.roll(b, ncp - 1, 0)
    o_ref[0, 0, 0] = _dot(jax.nn.gelu(pre).astype(BF16), w2_ref[0]).astype(BF16)


def _compress(r, pos, w1, w2):
    B, _, G, ncp, _ = r.shape
    return pl.pallas_call(
        _compress_kernel,
        grid=(B, 2, G),
        in_specs=[
            pl.BlockSpec((1, 1, 1, ncp, 1024), lambda b, k, g: (b, k, g, 0, 0)),
            pl.BlockSpec((1, 2, 1, 1024), lambda b, k, g: (k, 0, 0, 0)),
            pl.BlockSpec((1, 2048, CMP_HIDDEN), lambda b, k, g: (k, 0, 0)),
            pl.BlockSpec((1, CMP_HIDDEN, HEAD_DIM), lambda b, k, g: (k, 0, 0)),
        ],
        out_specs=pl.BlockSpec((1, 1, 1, ncp, HEAD_DIM), lambda b, k, g: (b, k, g, 0, 0)),
        out_shape=jax.ShapeDtypeStruct((B, 2, G, ncp, HEAD_DIM), BF16),
        compiler_params=_params("parallel", "parallel", "parallel"),
        name="compress",
    )(r, pos, w1, w2)


def _rel_bucket_np(dist):
    n = np.maximum(dist, 0)
    nf = np.maximum(n, 1).astype(np.float64)
    large = MAX_EXACT + (np.log(nf / MAX_EXACT) / math.log(MAX_DISTANCE / MAX_EXACT)
                         * (N_BUCKETS - MAX_EXACT)).astype(np.int64)
    return np.where(n < MAX_EXACT, n, np.minimum(large, N_BUCKETS - 1)).astype(np.int32)


def _static_maps():
    qq = np.arange(QBLK)[:, None]
    kk = np.arange(WINDOW + QBLK)[None, :]
    dist = qq + WINDOW - kk
    wmap = np.where((dist >= 0) & (dist < WINDOW), _rel_bucket_np(dist), -1)
    u = np.arange(LANES)[None, :]
    dist_c = qq - CMP_STRIDE * (u - CNEAR_LO) - (CMP_LEN - 1)
    cmap = np.where((dist_c >= 0) & (u < CNEAR), _rel_bucket_np(dist_c), -1)
    cmap = np.where(u < CNEAR, cmap, -2)
    return wmap.astype(np.int32), cmap.astype(np.int32)


def _tables_kernel(bias_ref, wmap_ref, cmap_ref, tw_ref, tc_ref):
    h = pl.program_id(0)
    far = bias_ref[N_BUCKETS - 1, h]

    def lookup(m):
        out = jnp.where(m == -1, NEG_INF, 0.0).astype(F32)
        for b in range(N_BUCKETS):
            out = jnp.where(m == b, (bias_ref[b, h] - far) * LOG2E, out)
        return out

    tw_ref[0] = lookup(wmap_ref[...])
    cmap = cmap_ref[...]
    val = lookup(cmap)
    hi = val.astype(BF16)
    lo = (val - hi.astype(F32)).astype(BF16)
    col = lax.broadcasted_iota(jnp.int32, cmap.shape, 1)
    hi_part = jnp.where(col < CNEAR, hi.astype(F32), 0.0)
    lo_part = pltpu.roll(jnp.where(col < CNEAR, lo.astype(F32), 0.0), CNEAR, 1)
    one = jnp.where(col == 2 * CNEAR, 1.0, 0.0)
    tc_ref[0] = (hi_part + lo_part + one).astype(BF16)


def _bias_tables(rel_bias):
    wmap, cmap = _static_maps()
    nw = WINDOW + QBLK
    return pl.pallas_call(
        _tables_kernel,
        grid=(N_HEADS,),
        in_specs=[
            pl.BlockSpec(memory_space=pltpu.SMEM),
            pl.BlockSpec((QBLK, nw), lambda h: (0, 0)),
            pl.BlockSpec((QBLK, LANES), lambda h: (0, 0)),
        ],
        out_specs=[
            pl.BlockSpec((1, QBLK, nw), lambda h: (h, 0, 0)),
            pl.BlockSpec((1, QBLK, LANES), lambda h: (h, 0, 0)),
        ],
        out_shape=[
            jax.ShapeDtypeStruct((N_HEADS, QBLK, nw), F32),
            jax.ShapeDtypeStruct((N_HEADS, QBLK, LANES), BF16),
        ],
        compiler_params=_params("parallel"),
        name="bias_tables",
    )(rel_bias, jnp.asarray(wmap), jnp.asarray(cmap))


def _nsa_kernel(q_ref, kc_ref, vc_ref, ksa_ref, vs_ref, kwp_ref, vw_ref, gate_ref, tw_ref, tc_ref,
                ovl_ref, o_ref, qaug_ref, m_ref, acc_ref):
    i = pl.program_id(2)
    qt = q_ref[0, 0, 0]
    ncp = kc_ref.shape[2]
    nb = ovl_ref.shape[0]
    off = pl.multiple_of(i * QBLK, QBLK)
    col_max = lambda a: jnp.max(a, axis=0, keepdims=True)

    c = lax.broadcasted_iota(jnp.int32, (ncp, LANES), 0)
    u = lax.broadcasted_iota(jnp.int32, (ncp, LANES), 1)
    rel = c - (CMP_STRIDE * i - CNEAR_LO)
    place = jnp.where((rel == u) & (u < CNEAR), 1.0, 0.0)
    place = jnp.where((rel == u - CNEAR) & (u >= CNEAR) & (u < 2 * CNEAR), 1.0, place)
    place = jnp.where((u == 2 * CNEAR) & (rel >= CNEAR), NEG_BF16, place)
    lhs = jnp.concatenate([kc_ref[0, 0], place.astype(BF16)], axis=1)
    s = _dot(lhs, jnp.concatenate([qt, tc_ref[0]], axis=0))
    p = jnp.exp2(s - jnp.maximum(col_max(s), M_INIT))
    l = jnp.sum(p, axis=0, keepdims=True)
    pn = p * (1.0 / jnp.maximum(l, 1e-30))
    o_c = _dot(vc_ref[0, 0], pn.astype(BF16))

    nw = WINDOW + QBLK
    sw = _dot(kwp_ref[0, 0, pl.ds(off, nw), :], qt) + tw_ref[0]
    pw = jnp.exp2(sw - col_max(sw))
    acc_w = _dot(vw_ref[0, 0, :, pl.ds(off, nw)], pw.astype(BF16))

    psum = pn[:, 0:QBLK] + pn[:, QBLK:2 * QBLK] + pn[:, 2 * QBLK:3 * QBLK] + pn[:, 3 * QBLK:4 * QBLK]
    p_hi = psum.astype(BF16)
    p_lo = (psum - p_hi.astype(F32)).astype(BF16)
    imp = _dot(ovl_ref[...], p_hi) + _dot(ovl_ref[...], p_lo)
    jb = lax.broadcasted_iota(jnp.int32, (nb, QBLK), 0)
    qq = lax.broadcasted_iota(jnp.int32, (nb, QBLK), 1)
    cur = (QBLK // SEL_BLOCK) * i + (qq >> 6)
    forced = (jb == 0) | (jb == cur) | (jb == cur - 1)
    sc = jnp.where(forced, TAKEN, jnp.where(jb <= cur, imp, NEG_INF))
    jio = jb.astype(F32)
    for _ in range(N_SELECT - N_FORCED):
        idx = jnp.min(jnp.where(sc == col_max(sc), jio, float(nb)), axis=0, keepdims=True)
        sc = jnp.where(jio == idx, TAKEN, sc)
    msel = jnp.where(sc == TAKEN, 0.0, NEG_BF16).astype(BF16)
    qaug_ref[0:nb, :] = jnp.concatenate([msel] * Q_PER_KV, axis=1)
    qaug_ref[nb:nb + LANES, :] = qt

    sn = _dot(ksa_ref[0, 0, pl.ds(off, 2 * QBLK), :], qaug_ref[...]) + tw_ref[0, WINDOW - QBLK:WINDOW + QBLK, :]
    m0 = col_max(sn)
    acc_ref[...] = _dot(vs_ref[0, 0, :, pl.ds(off, 2 * QBLK)], jnp.exp2(sn - m0).astype(BF16))
    m_ref[...] = jnp.broadcast_to(m0, m_ref.shape)

    n_far = jnp.maximum(i - 1, 0)

    def far_body(k, carry):
        offs = [pl.multiple_of(jnp.where(FAR_UNROLL * k + h < n_far, (FAR_UNROLL * k + h + 1) * KTILE, 0), KTILE)
                for h in range(FAR_UNROLL)]
        logits = [_dot(ksa_ref[0, 0, pl.ds(o, KTILE), :], qaug_ref[...]) for o in offs]
        m_run = m_ref[...]
        acc = acc_ref[...]
        for o, sl in zip(offs, logits):
            m_next = jnp.maximum(m_run, col_max(sl))
            alpha = jnp.exp2(m_run - m_next)
            pt = jnp.exp2(sl - m_next[0:1, :])
            acc = alpha[0:1, :] * acc + _dot(vs_ref[0, 0, :, pl.ds(o, KTILE)], pt.astype(BF16))
            m_run = m_next
        acc_ref[...] = acc
        m_ref[...] = m_run
        return carry

    lax.fori_loop(0, (n_far + FAR_UNROLL - 1) // FAR_UNROLL, far_body, 0)
    acc_s = acc_ref[...]

    g = gate_ref[0, 0, 0]
    out = (g[0:1, :] * o_c
           + (g[1:2, :] / acc_s[HEAD_DIM:HEAD_DIM + 1, :]) * acc_s[0:HEAD_DIM, :]
           + (g[2:3, :] / acc_w[HEAD_DIM:HEAD_DIM + 1, :]) * acc_w[0:HEAD_DIM, :])
    o_ref[0, 0, 0] = out.astype(BF16)


def _nsa(qt, kc, vct, ksa, vst, kwp, vwt, gates, twt, tct, ovlt):
    B, G, NQ, _, _ = qt.shape
    nb = ovlt.shape[0]
    per_bg = lambda a: pl.BlockSpec((1, 1) + a.shape[2:], lambda b, g, i: (b, g) + (0,) * (a.ndim - 2))
    per_g = lambda a: pl.BlockSpec((1,) + a.shape[1:], lambda b, g, i: (g,) + (0,) * (a.ndim - 1))
    blk = lambda n: pl.BlockSpec((1, 1, 1, n, QROWS), lambda b, g, i: (b, g, i, 0, 0))
    return pl.pallas_call(
        _nsa_kernel,
        grid=(B, G, NQ),
        in_specs=[blk(LANES), per_bg(kc), per_bg(vct), per_bg(ksa), per_bg(vst), per_bg(kwp), per_bg(vwt),
                  blk(N_BRANCH), per_g(twt), per_g(tct), pl.BlockSpec(ovlt.shape, lambda b, g, i: (0, 0))],
        out_specs=blk(HEAD_DIM),
        out_shape=jax.ShapeDtypeStruct((B, G, NQ, HEAD_DIM, QROWS), BF16),
        scratch_shapes=[pltpu.VMEM((nb + LANES, QROWS), BF16), pltpu.VMEM((8, QROWS), F32),
                        pltpu.VMEM((VROWS, QROWS), F32)],
        compiler_params=_params("parallel", "parallel", "arbitrary"),
        name="nsa",
    )(qt, kc, vct, ksa, vst, kwp, vwt, gates, twt, tct, ovlt)


def _out_proj_kernel(x_ref, yp_ref, yc_ref, yn_ref, w_ref, g_ref, o_ref):
    mix = (_dot(yp_ref[...], w_ref[0:256, :]) + _dot(yc_ref[...], w_ref[256:512, :])
           + _dot(yn_ref[...], w_ref[512:1024, :]))
    o_ref[...] = x_ref[...] + _rms(mix, g_ref[...])


def _out_proj(xf, yp, yc, yn, w, g):
    T = xf.shape[0]
    row = lambda n: pl.BlockSpec((ROW_TILE, n), lambda i: (i, 0))
    full = lambda a: pl.BlockSpec(a.shape, lambda i: (0,) * a.ndim)
    return pl.pallas_call(
        _out_proj_kernel,
        grid=(T // ROW_TILE,),
        in_specs=[row(D_MODEL), row(256), row(256), row(512), full(w), full(g)],
        out_specs=row(D_MODEL),
        out_shape=jax.ShapeDtypeStruct((T, D_MODEL), F32),
        compiler_params=_params("parallel"),
        name="out_proj",
    )(xf, yp, yc, yn, w, g)


def _mem_kv_kernel(mem_ref, g_ref, wk_ref, wv_ref, k_ref, v_ref):
    m = _rms(mem_ref[0], g_ref[...]).astype(BF16)
    k_ref[0] = _dot(m, wk_ref[...]).astype(BF16)
    v_ref[0] = _dot(m, wv_ref[...]).astype(BF16)


def _mem_kv(mem, g, wk, wv):
    B, M, _ = mem.shape
    full = lambda a: pl.BlockSpec(a.shape, lambda b: (0,) * a.ndim)
    per_b = pl.BlockSpec((1, M, D_MODEL), lambda b: (b, 0, 0))
    return pl.pallas_call(
        _mem_kv_kernel,
        grid=(B,),
        in_specs=[per_b, full(g), full(wk), full(wv)],
        out_specs=[per_b, per_b],
        out_shape=[jax.ShapeDtypeStruct((B, M, D_MODEL), BF16)] * 2,
        compiler_params=_params("parallel"),
        name="mem_kv",
    )(mem, g, wk, wv)


def _xattn_kernel(x_ref, gpre_ref, wq_ref, k_ref, v_ref, wo_ref, gpost_ref, o_ref):
    x = x_ref[0]
    h = _rms(x, gpre_ref[...]).astype(BF16)
    qx = (_dot(h, wq_ref[...]) * (XA_HEAD_DIM ** -0.5)).astype(BF16)
    outs = []
    for hh in range(XA_HEADS):
        sl = slice(hh * XA_HEAD_DIM, (hh + 1) * XA_HEAD_DIM)
        s = _dot_nt(qx[:, sl], k_ref[0, :, sl])
        p = jnp.exp(s - jnp.max(s, axis=1, keepdims=True))
        p = p * (1.0 / jnp.sum(p, axis=1, keepdims=True))
        outs.append(_dot(p.astype(BF16), v_ref[0, :, sl]).astype(BF16))
    o = jnp.concatenate(outs, axis=1)
    o_ref[0] = x + _rms(_dot(o, wo_ref[...]), gpost_ref[...])


def _xattn(x, gpre, wq, km, vm, wo, gpost):
    B, S, _ = x.shape
    M = km.shape[1]
    ts = ROW_TILE
    full = lambda a: pl.BlockSpec(a.shape, lambda b, s: (0,) * a.ndim)
    seq = pl.BlockSpec((1, ts, D_MODEL), lambda b, s: (b, s, 0))
    per_b = pl.BlockSpec((1, M, D_MODEL), lambda b, s: (b, 0, 0))
    return pl.pallas_call(
        _xattn_kernel,
        grid=(B, S // ts),
        in_specs=[seq, full(gpre), full(wq), per_b, per_b, full(wo), full(gpost)],
        out_specs=seq,
        out_shape=jax.ShapeDtypeStruct((B, S, D_MODEL), F32),
        compiler_params=_params("parallel", "parallel"),
        name="xattn",
    )(x, gpre, wq, km, vm, wo, gpost)


def _mlp_kernel(x_ref, gpre_ref, w1_ref, w2_ref, gpost_ref, o_ref):
    x = x_ref[...]
    h = _rms(x, gpre_ref[...]).astype(BF16)
    y = jnp.zeros(x.shape, F32)
    chunk = 1024
    for c in range(D_FF // chunk):
        a = jnp.maximum(_dot(h, w1_ref[:, c * chunk:(c + 1) * chunk]), 0.0)
        y = y + _dot((a * a).astype(BF16), w2_ref[c * chunk:(c + 1) * chunk, :])
    o_ref[...] = x + _rms(y, gpost_ref[...])


def _mlp(xf, gpre, w1, w2, gpost):
    T = xf.shape[0]
    row = pl.BlockSpec((ROW_TILE, D_MODEL), lambda i: (i, 0))
    full = lambda a: pl.BlockSpec(a.shape, lambda i: (0,) * a.ndim)
    return pl.pallas_call(
        _mlp_kernel,
        grid=(T // ROW_TILE,),
        in_specs=[row, full(gpre), full(w1), full(w2), full(gpost)],
        out_specs=row,
        out_shape=jax.ShapeDtypeStruct((T, D_MODEL), F32),
        compiler_params=_params("parallel"),
        name="mlp",
    )(xf, gpre, w1, w2, gpost)


def _heads_major(a, B, S):
    return a.reshape(B, S, N_KV, HEAD_DIM).transpose(0, 2, 1, 3)


def kernel(x, mem, rel_bias, mix_pre_g, mix_post_g, w_in, pool_w, pool_scale, conv_w, conv_b, conv_ln_g,
           conv_ln_b, conv_pw, cmp_k_pos, cmp_k_w1, cmp_k_w2, cmp_v_pos, cmp_v_w1, cmp_v_w2, w_out,
           xa_pre_g, xa_post_g, mem_g, xa_wq, xa_wk, xa_wv, xa_wo, mlp_pre_g, mlp_post_g, mlp_w1, mlp_w2):
    B, S, _ = x.shape
    assert S % ROW_TILE == 0 and S // SEL_BLOCK >= N_SELECT and S // SEL_BLOCK <= LANES
    T = B * S
    NQ = S // QBLK
    ncp = S // CMP_STRIDE
    nb = LANES
    row2 = lambda a: a.reshape(1, -1)

    tw, tc = _bias_tables(rel_bias)
    twt = tw.reshape(N_KV, QROWS, WINDOW + QBLK).transpose(0, 2, 1)
    tct = tc.reshape(N_KV, QROWS, LANES).transpose(0, 2, 1)

    pos = np.arange(S)
    blk_onehot = jnp.asarray((pos[:, None] // SEL_BLOCK == np.arange(nb)[None, :]), BF16)
    cs = np.arange(ncp)[:, None] * CMP_STRIDE
    ss = np.arange(nb)[None, :] * SEL_BLOCK
    ovl = (cs < ss + SEL_BLOCK) & (cs + CMP_LEN > ss) & (np.arange(ncp)[:, None] < ncp - 1) & (ss < S)
    ovlt = jnp.asarray(ovl.T, BF16)
    lane_ind = np.arange(LANES - HEAD_DIM) == 0
    neg_rows = jnp.asarray(np.where(lane_ind, NEG_BF16, 0.0)[:, None], BF16)
    pad_rows_w = jnp.asarray(np.concatenate([np.zeros(HEAD_DIM), lane_ind]), BF16)
    pad_rows_s = jnp.asarray(np.concatenate([np.zeros(nb + HEAD_DIM), lane_ind]), BF16)
    one_rows = jnp.asarray((np.arange(VROWS - HEAD_DIM) == 0)[:, None], BF16)

    def values_t(v, front):
        vt = jnp.concatenate([v.transpose(0, 1, 3, 2),
                              jnp.broadcast_to(one_rows, (B, N_KV, VROWS - HEAD_DIM, S))], axis=2)
        return jnp.pad(vt, ((0, 0), (0, 0), (0, 0), (front, 0)))

    xf = x.reshape(T, D_MODEL)
    for l in range(DEPTH):
        w_in_l = jnp.pad(w_in[l], ((0, 0), (0, 2176 - w_in.shape[2]))).astype(BF16)
        up, uc, q, kvc, kvs, gates = _in_proj(xf, row2(mix_pre_g[l]), w_in_l)

        pw_bd = jnp.zeros((POOL_WIDTH, POOL_WIDTH), F32)
        for g in range(len(POOL_WINDOWS)):
            pw_bd = lax.dynamic_update_slice(pw_bd, pool_w[l, g], (g * POOL_GROUP, g * POOL_GROUP))
        yp, yc = _mixers(up.reshape(B, S, 256), uc.reshape(B, S, 512), pw_bd.astype(BF16),
                         row2(pool_scale[l]), conv_w[l], row2(conv_b[l]), row2(conv_ln_g[l]),
                         row2(conv_ln_b[l]), conv_pw[l].astype(BF16))

        rk = _heads_major(kvc[:, 0:128], B, S).reshape(B, N_KV, ncp, CMP_STRIDE * HEAD_DIM)
        rv = _heads_major(kvc[:, 128:256], B, S).reshape(B, N_KV, ncp, CMP_STRIDE * HEAD_DIM)
        cpos = jnp.stack([cmp_k_pos[l], cmp_v_pos[l]]).reshape(2, 2, 1, CMP_STRIDE * HEAD_DIM)
        cw1 = jnp.stack([cmp_k_w1[l], cmp_v_w1[l]]).astype(BF16)
        cw2 = jnp.stack([cmp_k_w2[l], cmp_v_w2[l]]).astype(BF16)
        kcv = _compress(jnp.stack([rk, rv], axis=1), cpos, cw1, cw2)
        kc = jnp.pad(kcv[:, 0], ((0, 0),) * 3 + ((0, LANES - HEAD_DIM),))
        vct = kcv[:, 1].transpose(0, 1, 3, 2)

        qt = q.reshape(B, NQ, QBLK, N_KV, Q_PER_KV, HEAD_DIM).transpose(0, 3, 1, 5, 4, 2)
        qt = qt.reshape(B, N_KV, NQ, HEAD_DIM, QROWS)
        qt = jnp.concatenate([qt, jnp.broadcast_to(neg_rows, (B, N_KV, NQ, LANES - HEAD_DIM, QROWS))], axis=3)
        ga = gates[:, 0:N_HEADS * N_BRANCH].reshape(B, NQ, QBLK, N_KV, Q_PER_KV, N_BRANCH)
        ga = ga.transpose(0, 3, 1, 5, 4, 2).reshape(B, N_KV, NQ, N_BRANCH, QROWS)
        k_s, v_s, k_w, v_w = [_heads_major(kvs[:, j * 128:(j + 1) * 128], B, S) for j in range(4)]
        zeros = jnp.zeros((B, N_KV, S, LANES - HEAD_DIM), BF16)
        ksa = jnp.concatenate([jnp.broadcast_to(blk_onehot, (B, N_KV, S, nb)), k_s, zeros], axis=-1)
        ksa = jnp.concatenate([jnp.broadcast_to(pad_rows_s, (B, N_KV, QBLK, nb + LANES)), ksa], axis=2)
        kwp = jnp.concatenate([jnp.broadcast_to(pad_rows_w, (B, N_KV, WINDOW, LANES)),
                               jnp.concatenate([k_w, zeros], axis=-1)], axis=2)
        yn = _nsa(qt, kc, vct, ksa, values_t(v_s, QBLK), kwp, values_t(v_w, WINDOW), ga, twt, tct, ovlt)
        yn = yn.reshape(B, N_KV, NQ, HEAD_DIM, Q_PER_KV, QBLK).transpose(0, 2, 5, 1, 4, 3).reshape(T, NSA_WIDTH)

        xf = _out_proj(xf, yp.reshape(T, 256), yc.reshape(T, 256), yn, w_out[l].astype(BF16),
                       row2(mix_post_g[l]))

        km, vm = _mem_kv(mem, row2(mem_g[l]), xa_wk[l].astype(BF16), xa_wv[l].astype(BF16))
        xf = _xattn(xf.reshape(B, S, D_MODEL), row2(xa_pre_g[l]), xa_wq[l].astype(BF16), km, vm,
                    xa_wo[l].astype(BF16), row2(xa_post_g[l])).reshape(T, D_MODEL)

        xf = _mlp(xf, row2(mlp_pre_g[l]), mlp_w1[l].astype(BF16), mlp_w2[l].astype(BF16),
                  row2(mlp_post_g[l]))
    return xf.reshape(B, S, D_MODEL)
```

```python
import functools
import math

import numpy as np
import jax
import jax.numpy as jnp
from jax import lax
from jax.experimental import pallas as pl
from jax.experimental.pallas import tpu as pltpu

F32 = jnp.float32
BF16 = jnp.bfloat16

D_MODEL = 1024
DEPTH = 2
POOL_WIDTH = 256
POOL_WINDOWS = (2, 4, 8, 16)
POOL_GROUP = 64
CONV_WIDTH = 256
CONV_KSIZE = 31
NSA_WIDTH = 512
HEAD_DIM = 64
N_HEADS = 8
N_KV = 2
Q_PER_KV = 4
CMP_LEN = 32
CMP_STRIDE = 16
CMP_HIDDEN = 256
SEL_BLOCK = 64
N_SELECT = 16
WINDOW = 512
N_BRANCH = 3
N_BUCKETS = 32
MAX_EXACT = 16
MAX_DISTANCE = 128
XA_HEADS = 4
XA_HEAD_DIM = 256
D_FF = 4096
EPS = 1e-6
NEG_INF = -1e30
FORCE = 1e30

LANES = 128
ROW_TILE = 512
QBLK = 256
KTILE = 256
QROWS = Q_PER_KV * QBLK
HALO = 32
CNEAR = 24
CNEAR_LO = 8
NEG_BF16 = -(2.0 ** 100)
M_INIT = -1e29
TAKEN = -3e38
N_FORCED = 3
LOG2E = 1.4426950408889634
FAR_UNROLL = 2
VROWS = 80
VMEM_LIMIT = 56 * 1024 * 1024


def _rms(x, g):
    return x * lax.rsqrt(jnp.mean(x * x, axis=-1, keepdims=True) + EPS) * g


def _dot(a, b):
    return jnp.dot(a, b, preferred_element_type=F32)


def _dot_nt(a, b):
    return lax.dot_general(a, b, (((1,), (1,)), ((), ())), preferred_element_type=F32)


def _params(*sem):
    return pltpu.CompilerParams(dimension_semantics=sem, vmem_limit_bytes=VMEM_LIMIT)


def _in_proj_weights(w):
    z64 = jnp.zeros((D_MODEL, LANES - HEAD_DIM), w.dtype)
    z4 = jnp.zeros((D_MODEL, 4), w.dtype)
    kv0 = POOL_WIDTH + 2 * CONV_WIDTH + NSA_WIDTH
    col = lambda j, g: w[:, kv0 + j * 128 + g * HEAD_DIM:kv0 + j * 128 + (g + 1) * HEAD_DIM]
    w_a = jnp.concatenate([w[:, 0:768], col(2, 0), z64, col(2, 1), z64, col(4, 0), z64, col(4, 1), z64,
                           w[:, kv0:kv0 + 256]], axis=1)
    gw = w[:, kv0 + 768:]
    w_b = jnp.concatenate([w[:, 768:1280], w[:, kv0 + 384:kv0 + 512], w[:, kv0 + 640:kv0 + 768],
                           gw[:, 0:12], z4, gw[:, 12:24], z4], axis=1)
    return w_a.astype(BF16), w_b.T.astype(BF16)


def _in_proj_kernel(x_ref, g_ref, wa_ref, wbt_ref, ks_in, kw_in, vs_in, vw_in,
                    pool_ref, conv_ref, kc_ref, ks_ref, kw_ref, qt_ref, vs_ref, vw_ref, gate_ref):
    del ks_in, kw_in, vs_in, vw_in
    tm = x_ref.shape[1]
    h = _rms(x_ref[0], g_ref[...]).astype(BF16)
    pool_ref[0] = _dot(h, wa_ref[:, 0:256])
    conv_ref[0] = _dot(h, wa_ref[:, 256:768])
    ksw = _dot(h, wa_ref[:, 768:1280]).astype(BF16)
    kvc = _dot(h, wa_ref[:, 1280:1536])
    for g in range(N_KV):
        ks_ref[0, g] = ksw[:, g * LANES:(g + 1) * LANES]
        kw_ref[0, g] = ksw[:, (N_KV + g) * LANES:(N_KV + g + 1) * LANES]
        for kv in range(2):
            kc_ref[0, kv, g] = kvc[:, (kv * N_KV + g) * HEAD_DIM:(kv * N_KV + g + 1) * HEAD_DIM]

    pt = _dot_nt(wbt_ref[...], h)
    qs = (pt[0:NSA_WIDTH] * (HEAD_DIM ** -0.5 * LOG2E)).astype(BF16)
    ones_row = jnp.where(lax.broadcasted_iota(jnp.int32, (VROWS - HEAD_DIM, tm), 0) == 0, 1.0, 0.0).astype(BF16)
    gs = jax.nn.sigmoid(pt[NSA_WIDTH + 256:NSA_WIDTH + 288])
    for g in range(N_KV):
        for qb in range(tm // QBLK):
            cols = slice(qb * QBLK, (qb + 1) * QBLK)
            qt_ref[0, g, qb] = jnp.concatenate(
                [qs[(g * Q_PER_KV + r) * HEAD_DIM:(g * Q_PER_KV + r + 1) * HEAD_DIM, cols] for r in range(Q_PER_KV)],
                axis=1)
            gate_ref[0, qb, g] = gs[g * 16:g * 16 + Q_PER_KV * N_BRANCH, cols]
        v0 = NSA_WIDTH + g * HEAD_DIM
        vs_ref[0, g] = jnp.concatenate([pt[v0:v0 + HEAD_DIM].astype(BF16), ones_row], axis=0)
        vw_ref[0, g] = jnp.concatenate([pt[v0 + 128:v0 + 128 + HEAD_DIM].astype(BF16), ones_row], axis=0)


def _in_proj(x, g, w_a, w_bt, ks0, kw0, vs0, vw0):
    B, S, _ = x.shape
    tm = ROW_TILE
    nq = tm // QBLK
    full = lambda a: pl.BlockSpec(a.shape, lambda b, s: (0,) * a.ndim)
    anyspec = pl.BlockSpec(memory_space=pl.ANY)
    seq = lambda n: pl.BlockSpec((1, tm, n), lambda b, s: (b, s, 0))
    keys = pl.BlockSpec((1, N_KV, tm, LANES), lambda b, s: (b, 0, s, 0))
    vals = pl.BlockSpec((1, N_KV, VROWS, tm), lambda b, s: (b, 0, 0, s))
    return pl.pallas_call(
        _in_proj_kernel,
        grid=(B, S // tm),
        in_specs=[seq(D_MODEL), full(g), full(w_a), full(w_bt), anyspec, anyspec, anyspec, anyspec],
        out_specs=[
            seq(POOL_WIDTH), seq(2 * CONV_WIDTH),
            pl.BlockSpec((1, 2, N_KV, tm, HEAD_DIM), lambda b, s: (b, 0, 0, s, 0)),
            keys, keys,
            pl.BlockSpec((1, N_KV, nq, HEAD_DIM, QROWS), lambda b, s: (b, 0, s, 0, 0)),
            vals, vals,
            pl.BlockSpec((1, nq, N_KV, Q_PER_KV * N_BRANCH, QBLK), lambda b, s: (b, s, 0, 0, 0)),
        ],
        out_shape=[
            jax.ShapeDtypeStruct((B, S, POOL_WIDTH), F32),
            jax.ShapeDtypeStruct((B, S, 2 * CONV_WIDTH), F32),
            jax.ShapeDtypeStruct((B, 2, N_KV, S, HEAD_DIM), F32),
            jax.ShapeDtypeStruct(ks0.shape, BF16),
            jax.ShapeDtypeStruct(kw0.shape, BF16),
            jax.ShapeDtypeStruct((B, N_KV, S // QBLK, HEAD_DIM, QROWS), BF16),
            jax.ShapeDtypeStruct(vs0.shape, BF16),
            jax.ShapeDtypeStruct(vw0.shape, BF16),
            jax.ShapeDtypeStruct((B, S // QBLK, N_KV, Q_PER_KV * N_BRANCH, QBLK), F32),
        ],
        input_output_aliases={4: 3, 5: 4, 6: 6, 7: 7},
        compiler_params=_params("parallel", "parallel"),
        name="in_proj",
    )(x, g, w_a, w_bt, ks0, kw0, vs0, vw0)


def _mixers_kernel(up_ref, uc_ref, pw_ref, ps_ref, cw_ref, cb_ref, lg_ref, lb_ref, cpw_ref,
                   yp_ref, yc_ref, ubuf, hbuf):
    s = pl.program_id(1)
    ts = up_ref.shape[1]

    @pl.when(s == 0)
    def _():
        ubuf[0:HALO, :] = jnp.zeros((HALO, POOL_WIDTH), F32)
        hbuf[0:HALO, :] = jnp.zeros((HALO, CONV_WIDTH), F32)

    u = up_ref[0]
    ubuf[HALO:HALO + ts, :] = u
    back = lambda k, c0, c1: ubuf[HALO - k:HALO - k + ts, c0:c1]
    lo = u[:, 0:128] + back(1, 0, 128)
    s4 = lo + back(2, 0, 128) + back(3, 0, 128)
    hi = u[:, 128:256]
    for k in range(1, 8):
        hi = hi + back(k, 128, 256)
    s16 = hi
    for k in range(8, 16):
        s16 = s16 + back(k, 128, 256)
    t1 = (s * ts + lax.broadcasted_iota(jnp.int32, (ts, 128), 0) + 1).astype(F32)
    lane = lax.broadcasted_iota(jnp.int32, (ts, 128), 1)
    first = lane < POOL_GROUP
    sum_lo = jnp.where(first, lo, s4)
    den_lo = jnp.minimum(t1, jnp.where(first, 2.0, 4.0))
    sum_hi = jnp.where(first, hi, s16)
    den_hi = jnp.minimum(t1, jnp.where(first, 8.0, 16.0))
    d = jnp.concatenate([sum_lo / den_lo - u[:, 0:128], sum_hi / den_hi - u[:, 128:256]], axis=1)
    yp_ref[0] = (_dot(d.astype(BF16), pw_ref[...]) * ps_ref[...]).astype(BF16)
    ubuf[0:HALO, :] = ubuf[ts:ts + HALO, :]

    uc = uc_ref[0]
    hbuf[HALO:HALO + ts, :] = uc[:, 0:CONV_WIDTH] * jax.nn.sigmoid(uc[:, CONV_WIDTH:])
    acc = jnp.zeros((ts, CONV_WIDTH), F32) + cb_ref[...]
    for j in range(CONV_KSIZE):
        o = HALO - (CONV_KSIZE - 1) + j
        acc = acc + hbuf[o:o + ts, :] * cw_ref[j:j + 1, :]
    mu = jnp.mean(acc, axis=-1, keepdims=True)
    var = jnp.mean(jnp.square(acc - mu), axis=-1, keepdims=True)
    y = (acc - mu) * lax.rsqrt(var + EPS) * lg_ref[...] + lb_ref[...]
    y = y * jax.nn.sigmoid(y)
    yc_ref[0] = _dot(y.astype(BF16), cpw_ref[...]).astype(BF16)
    hbuf[0:HALO, :] = hbuf[ts:ts + HALO, :]


def _mixers(up, uc, pw, ps, cw, cb, lg, lb, cpw):
    B, S, _ = up.shape
    ts = ROW_TILE
    seq = lambda n: pl.BlockSpec((1, ts, n), lambda b, s: (b, s, 0))
    full = lambda a: pl.BlockSpec(a.shape, lambda b, s: (0,) * a.ndim)
    return pl.pallas_call(
        _mixers_kernel,
        grid=(B, S // ts),
        in_specs=[seq(256), seq(512)] + [full(a) for a in (pw, ps, cw, cb, lg, lb, cpw)],
        out_specs=[seq(256), seq(256)],
        out_shape=[jax.ShapeDtypeStruct((B, S, 256), BF16)] * 2,
        scratch_shapes=[pltpu.VMEM((HALO + ts, POOL_WIDTH), F32), pltpu.VMEM((HALO + ts, CONV_WIDTH), F32)],
        compiler_params=_params("parallel", "arbitrary"),
        name="mixers",
    )(up, uc, pw, ps, cw, cb, lg, lb, cpw)


def _compress_kernel(r_ref, pos_ref, w1_ref, w2k_ref, w2vt_ref, kc_ref, vct_ref):
    ncp = r_ref.shape[3]
    half = CMP_STRIDE * HEAD_DIM

    def hidden(kv):
        r = r_ref[0, kv, 0]
        a = _dot((r + pos_ref[kv, 0]).astype(BF16), w1_ref[kv, 0:half, :])
        b = _dot((r + pos_ref[kv, 1]).astype(BF16), w1_ref[kv, half:2 * half, :])
        return jax.nn.gelu(a + pltpu.roll(b, ncp - 1, 0)).astype(BF16)

    kc = _dot(hidden(0), w2k_ref[...]).astype(BF16)
    kc_ref[0, 0] = jnp.concatenate([kc, jnp.zeros((ncp, LANES - HEAD_DIM), BF16)], axis=1)
    vct_ref[0, 0] = _dot_nt(w2vt_ref[...], hidden(1)).astype(BF16)


def _compress(r, pos, w1, w2k, w2vt):
    B, _, G, ncp, _ = r.shape
    full = lambda a: pl.BlockSpec(a.shape, lambda b, g: (0,) * a.ndim)
    return pl.pallas_call(
        _compress_kernel,
        grid=(B, G),
        in_specs=[pl.BlockSpec((1, 2, 1, ncp, CMP_STRIDE * HEAD_DIM), lambda b, g: (b, 0, g, 0, 0)),
                  full(pos), full(w1), full(w2k), full(w2vt)],
        out_specs=[pl.BlockSpec((1, 1, ncp, LANES), lambda b, g: (b, g, 0, 0)),
                   pl.BlockSpec((1, 1, HEAD_DIM, ncp), lambda b, g: (b, g, 0, 0))],
        out_shape=[jax.ShapeDtypeStruct((B, G, ncp, LANES), BF16),
                   jax.ShapeDtypeStruct((B, G, HEAD_DIM, ncp), BF16)],
        compiler_params=_params("parallel", "parallel"),
        name="compress",
    )(r, pos, w1, w2k, w2vt)


def _rel_bucket_np(dist):
    n = np.maximum(dist, 0)
    nf = np.maximum(n, 1).astype(np.float64)
    large = MAX_EXACT + (np.log(nf / MAX_EXACT) / math.log(MAX_DISTANCE / MAX_EXACT)
                         * (N_BUCKETS - MAX_EXACT)).astype(np.int64)
    return np.where(n < MAX_EXACT, n, np.minimum(large, N_BUCKETS - 1)).astype(np.int32)


def _static_maps():
    qq = np.arange(QBLK)[:, None]
    kk = np.arange(WINDOW + QBLK)[None, :]
    dist = qq + WINDOW - kk
    wmap = np.where((dist >= 0) & (dist < WINDOW), _rel_bucket_np(dist), -1)
    u = np.arange(LANES)[None, :]
    dist_c = qq - CMP_STRIDE * (u - CNEAR_LO) - (CMP_LEN - 1)
    cmap = np.where((dist_c >= 0) & (u < CNEAR), _rel_bucket_np(dist_c), -1)
    cmap = np.where(u < CNEAR, cmap, -2)
    return wmap.astype(np.int32), cmap.astype(np.int32)


def _tables_kernel(bias_ref, wmap_ref, cmap_ref, tw_ref, tc_ref):
    h = pl.program_id(0)
    far = bias_ref[N_BUCKETS - 1, h]

    def lookup(m):
        out = jnp.where(m == -1, NEG_INF, 0.0).astype(F32)
        for b in range(N_BUCKETS):
            out = jnp.where(m == b, (bias_ref[b, h] - far) * LOG2E, out)
        return out

    tw_ref[0] = lookup(wmap_ref[...])
    cmap = cmap_ref[...]
    val = lookup(cmap)
    hi = val.astype(BF16)
    lo = (val - hi.astype(F32)).astype(BF16)
    col = lax.broadcasted_iota(jnp.int32, cmap.shape, 1)
    hi_part = jnp.where(col < CNEAR, hi.astype(F32), 0.0)
    lo_part = pltpu.roll(jnp.where(col < CNEAR, lo.astype(F32), 0.0), CNEAR, 1)
    one = jnp.where(col == 2 * CNEAR, 1.0, 0.0)
    tc_ref[0] = (hi_part + lo_part + one).astype(BF16)


def _bias_tables(rel_bias):
    wmap, cmap = _static_maps()
    nw = WINDOW + QBLK
    return pl.pallas_call(
        _tables_kernel,
        grid=(N_HEADS,),
        in_specs=[
            pl.BlockSpec(memory_space=pltpu.SMEM),
            pl.BlockSpec((QBLK, nw), lambda h: (0, 0)),
            pl.BlockSpec((QBLK, LANES), lambda h: (0, 0)),
        ],
        out_specs=[
            pl.BlockSpec((1, QBLK, nw), lambda h: (h, 0, 0)),
            pl.BlockSpec((1, QBLK, LANES), lambda h: (h, 0, 0)),
        ],
        out_shape=[
            jax.ShapeDtypeStruct((N_HEADS, QBLK, nw), F32),
            jax.ShapeDtypeStruct((N_HEADS, QBLK, LANES), BF16),
        ],
        compiler_params=_params("parallel"),
        name="bias_tables",
    )(rel_bias, jnp.asarray(wmap), jnp.asarray(cmap))


def _nsa_tables(S):
    ncp = S // CMP_STRIDE
    nb = LANES
    pos = np.arange(S + KTILE)
    eblk = (pos[:, None] // SEL_BLOCK == np.arange(nb)[None, :]) & (pos[:, None] < S)
    cs = np.arange(ncp)[None, :] * CMP_STRIDE
    ss = np.arange(nb)[:, None] * SEL_BLOCK
    ovlt = (cs < ss + SEL_BLOCK) & (cs + CMP_LEN > ss) & (np.arange(ncp)[None, :] < ncp - 1) & (ss < S)
    rel = np.arange(2 * ncp)[:, None] - ncp + CNEAR_LO
    u = np.arange(LANES)[None, :]
    place = np.where((rel == u) & (u < CNEAR), 1.0, 0.0)
    place = np.where((rel == u - CNEAR) & (u >= CNEAR) & (u < 2 * CNEAR), 1.0, place)
    place = np.where((u == 2 * CNEAR) & (rel >= CNEAR), NEG_BF16, place)
    return jnp.asarray(eblk, BF16), jnp.asarray(ovlt, BF16), jnp.asarray(place, BF16)


def _nsa_kernel(qt_ref, kc_ref, vc_ref, eb_ref, ks_ref, vs_ref, kw_ref, vw_ref, gate_ref, tw_ref, tc_ref,
                ovl_ref, place_ref, o_ref, qaug_ref, s0_ref, s1_ref, m_ref, acc_ref):
    i = pl.program_id(2)
    ncp = kc_ref.shape[2]
    nb = ovl_ref.shape[0]
    pad_tile = ks_ref.shape[2] // KTILE - 1
    col_max = lambda a: jnp.max(a, axis=0, keepdims=True)
    tile_off = lambda t, ok: pl.multiple_of(jnp.where(ok, t, pad_tile) * KTILE, KTILE)

    row = lax.broadcasted_iota(jnp.int32, (LANES - HEAD_DIM, QROWS), 0)
    qn = jnp.concatenate([qt_ref[0, 0, 0], jnp.where(row == 0, NEG_BF16, 0.0).astype(BF16)], axis=0)
    qaug_ref[nb:nb + LANES, :] = qn

    place = place_ref[pl.ds(pl.multiple_of(ncp - CMP_STRIDE * i, CMP_STRIDE), ncp), :]
    s = _dot(jnp.concatenate([kc_ref[0, 0], place], axis=1), jnp.concatenate([qn, tc_ref[0]], axis=0))
    p = jnp.exp2(s - jnp.maximum(col_max(s), M_INIT))
    inv_l = 1.0 / jnp.maximum(jnp.sum(p, axis=0, keepdims=True), 1e-30)
    pb = p.astype(BF16)
    o_c = _dot(vc_ref[0, 0], pb) * inv_l
    imp4 = _dot(ovl_ref[...], pb) * inv_l

    win = []
    for tt in range(WINDOW // KTILE + 1):
        t = i - WINDOW // KTILE + tt
        o = tile_off(t, t >= 0)
        win.append((o, _dot(kw_ref[0, 0, pl.ds(o, KTILE), :], qn) + tw_ref[0, tt * KTILE:(tt + 1) * KTILE, :]))
    mw = functools.reduce(jnp.maximum, [col_max(sw) for _, sw in win])
    acc_w = sum(_dot(vw_ref[0, 0, :, pl.ds(o, KTILE)], jnp.exp2(sw - mw).astype(BF16)) for o, sw in win)

    imp = imp4[:, 0:QBLK] + imp4[:, QBLK:2 * QBLK] + imp4[:, 2 * QBLK:3 * QBLK] + imp4[:, 3 * QBLK:4 * QBLK]
    jb = lax.broadcasted_iota(jnp.int32, (nb, QBLK), 0)
    qq = lax.broadcasted_iota(jnp.int32, (nb, QBLK), 1)
    cur = (QBLK // SEL_BLOCK) * i + (qq >> 6)
    forced = (jb == 0) | (jb == cur) | (jb == cur - 1)
    sc = jnp.where(forced, TAKEN, jnp.where(jb <= cur, imp, NEG_INF))
    jio = jb.astype(F32)
    for _ in range(N_SELECT - N_FORCED):
        idx = jnp.min(jnp.where(sc == col_max(sc), jio, float(nb)), axis=0, keepdims=True)
        sc = jnp.where(jio == idx, TAKEN, sc)
    msel = jnp.where(sc == TAKEN, 0.0, NEG_BF16).astype(BF16)
    qaug_ref[0:nb, :] = jnp.concatenate([msel] * Q_PER_KV, axis=1)

    def sel_logits(o):
        keys = jnp.concatenate([eb_ref[pl.ds(o, KTILE), :], ks_ref[0, 0, pl.ds(o, KTILE), :]], axis=1)
        return _dot(keys, qaug_ref[...])

    o_prev = tile_off(i - 1, i >= 1)
    o_diag = pl.multiple_of(i * KTILE, KTILE)
    s_prev = sel_logits(o_prev) + tw_ref[0, WINDOW - KTILE:WINDOW, :]
    s_diag = sel_logits(o_diag) + tw_ref[0, WINDOW:WINDOW + KTILE, :]
    m0 = jnp.maximum(col_max(s_prev), col_max(s_diag))
    acc_ref[...] = (_dot(vs_ref[0, 0, :, pl.ds(o_prev, KTILE)], jnp.exp2(s_prev - m0).astype(BF16))
                    + _dot(vs_ref[0, 0, :, pl.ds(o_diag, KTILE)], jnp.exp2(s_diag - m0).astype(BF16)))
    m_ref[...] = jnp.broadcast_to(m0, m_ref.shape)

    n_far = jnp.maximum(i - 1, 0)
    far_off = lambda t: tile_off(t, t < n_far)

    def update(s_ref, t):
        o = far_off(t)
        sl = s_ref[...]
        m_prev = m_ref[...]
        m_next = jnp.maximum(m_prev, col_max(sl))
        alpha = jnp.exp2(m_prev - m_next)
        pt = jnp.exp2(sl - m_next[0:1, :])
        acc_ref[...] = alpha[0:1, :] * acc_ref[...] + _dot(vs_ref[0, 0, :, pl.ds(o, KTILE)], pt.astype(BF16))
        m_ref[...] = m_next

    s0_ref[...] = sel_logits(far_off(0))

    def far_body(k, carry):
        s1_ref[...] = sel_logits(far_off(2 * k + 1))
        update(s0_ref, 2 * k)
        s0_ref[...] = sel_logits(far_off(2 * k + 2))
        update(s1_ref, 2 * k + 1)
        return carry

    lax.fori_loop(0, (n_far + 1) // 2, far_body, 0)
    acc_s = acc_ref[...]

    gb = gate_ref[0, 0, 0]
    gate = lambda br: jnp.concatenate(
        [gb[r * N_BRANCH + br:r * N_BRANCH + br + 1, :] for r in range(Q_PER_KV)], axis=1)
    out = (gate(0) * o_c
           + (gate(1) / acc_s[HEAD_DIM:HEAD_DIM + 1, :]) * acc_s[0:HEAD_DIM, :]
           + (gate(2) / acc_w[HEAD_DIM:HEAD_DIM + 1, :]) * acc_w[0:HEAD_DIM, :]).astype(BF16)
    for r in range(Q_PER_KV):
        o_ref[0, r * HEAD_DIM:(r + 1) * HEAD_DIM, :] = out[:, r * QBLK:(r + 1) * QBLK]


def _nsa(qt, kc, vct, eblk, ks, vst, kw, vwt, gates, twt, tct, ovlt, place):
    B, G, NQ, _, _ = qt.shape
    S = NQ * QBLK
    nb = ovlt.shape[0]
    per_bg = lambda a: pl.BlockSpec((1, 1) + a.shape[2:], lambda b, g, i: (b, g) + (0,) * (a.ndim - 2))
    per_g = lambda a: pl.BlockSpec((1,) + a.shape[1:], lambda b, g, i: (g,) + (0,) * (a.ndim - 1))
    const = lambda a: pl.BlockSpec(a.shape, lambda b, g, i: (0,) * a.ndim)
    return pl.pallas_call(
        _nsa_kernel,
        grid=(B, G, NQ),
        in_specs=[pl.BlockSpec((1, 1, 1, HEAD_DIM, QROWS), lambda b, g, i: (b, g, i, 0, 0)),
                  per_bg(kc), per_bg(vct), const(eblk), per_bg(ks), per_bg(vst), per_bg(kw), per_bg(vwt),
                  pl.BlockSpec((1, 1, 1, Q_PER_KV * N_BRANCH, QBLK), lambda b, g, i: (b, i, g, 0, 0)),
                  per_g(twt), per_g(tct), const(ovlt), const(place)],
        out_specs=pl.BlockSpec((1, Q_PER_KV * HEAD_DIM, QBLK), lambda b, g, i: (b, g, i)),
        out_shape=jax.ShapeDtypeStruct((B, NSA_WIDTH, S), BF16),
        scratch_shapes=[pltpu.VMEM((nb + LANES, QROWS), BF16), pltpu.VMEM((KTILE, QROWS), F32),
                        pltpu.VMEM((KTILE, QROWS), F32), pltpu.VMEM((8, QROWS), F32),
                        pltpu.VMEM((VROWS, QROWS), F32)],
        compiler_params=_params("parallel", "parallel", "arbitrary"),
        name="nsa",
    )(qt, kc, vct, eblk, ks, vst, kw, vwt, gates, twt, tct, ovlt, place)


def _out_proj_kernel(x_ref, yp_ref, yc_ref, ynt_ref, w_ref, g_ref, o_ref):
    nsa = lax.dot_general(ynt_ref[0], w_ref[512:1024, :], (((0,), (0,)), ((), ())), preferred_element_type=F32)
    mix = _dot(yp_ref[0], w_ref[0:256, :]) + _dot(yc_ref[0], w_ref[256:512, :]) + nsa
    o_ref[0] = x_ref[0] + _rms(mix, g_ref[...])


def _out_proj(x, yp, yc, ynt, w, g):
    B, S, _ = x.shape
    tm = ROW_TILE
    seq = lambda n: pl.BlockSpec((1, tm, n), lambda b, s: (b, s, 0))
    full = lambda a: pl.BlockSpec(a.shape, lambda b, s: (0,) * a.ndim)
    return pl.pallas_call(
        _out_proj_kernel,
        grid=(B, S // tm),
        in_specs=[seq(D_MODEL), seq(256), seq(256), pl.BlockSpec((1, NSA_WIDTH, tm), lambda b, s: (b, 0, s)),
                  full(w), full(g)],
        out_specs=seq(D_MODEL),
        out_shape=jax.ShapeDtypeStruct((B, S, D_MODEL), F32),
        compiler_params=_params("parallel", "parallel"),
        name="out_proj",
    )(x, yp, yc, ynt, w, g)


def _mem_kv_kernel(mem_ref, g_ref, wk_ref, wv_ref, k_ref, v_ref):
    m = _rms(mem_ref[0], g_ref[...]).astype(BF16)
    k_ref[0] = _dot(m, wk_ref[...]).astype(BF16)
    v_ref[0] = _dot(m, wv_ref[...]).astype(BF16)


def _mem_kv(mem, g, wk, wv):
    B, M, _ = mem.shape
    full = lambda a: pl.BlockSpec(a.shape, lambda b: (0,) * a.ndim)
    per_b = pl.BlockSpec((1, M, D_MODEL), lambda b: (b, 0, 0))
    return pl.pallas_call(
        _mem_kv_kernel,
        grid=(B,),
        in_specs=[per_b, full(g), full(wk), full(wv)],
        out_specs=[per_b, per_b],
        out_shape=[jax.ShapeDtypeStruct((B, M, D_MODEL), BF16)] * 2,
        compiler_params=_params("parallel"),
        name="mem_kv",
    )(mem, g, wk, wv)


def _xattn_kernel(x_ref, gpre_ref, wq_ref, k_ref, v_ref, wo_ref, gpost_ref, o_ref):
    x = x_ref[0]
    h = _rms(x, gpre_ref[...]).astype(BF16)
    qx = (_dot(h, wq_ref[...]) * (XA_HEAD_DIM ** -0.5)).astype(BF16)
    outs = []
    for hh in range(XA_HEADS):
        sl = slice(hh * XA_HEAD_DIM, (hh + 1) * XA_HEAD_DIM)
        s = _dot_nt(qx[:, sl], k_ref[0, :, sl])
        p = jnp.exp(s - jnp.max(s, axis=1, keepdims=True))
        p = p * (1.0 / jnp.sum(p, axis=1, keepdims=True))
        outs.append(_dot(p.astype(BF16), v_ref[0, :, sl]).astype(BF16))
    o = jnp.concatenate(outs, axis=1)
    o_ref[0] = x + _rms(_dot(o, wo_ref[...]), gpost_ref[...])


def _xattn(x, gpre, wq, km, vm, wo, gpost):
    B, S, _ = x.shape
    M = km.shape[1]
    ts = ROW_TILE
    full = lambda a: pl.BlockSpec(a.shape, lambda b, s: (0,) * a.ndim)
    seq = pl.BlockSpec((1, ts, D_MODEL), lambda b, s: (b, s, 0))
    per_b = pl.BlockSpec((1, M, D_MODEL), lambda b, s: (b, 0, 0))
    return pl.pallas_call(
        _xattn_kernel,
        grid=(B, S // ts),
        in_specs=[seq, full(gpre), full(wq), per_b, per_b, full(wo), full(gpost)],
        out_specs=seq,
        out_shape=jax.ShapeDtypeStruct((B, S, D_MODEL), F32),
        compiler_params=_params("parallel", "parallel"),
        name="xattn",
    )(x, gpre, wq, km, vm, wo, gpost)


def _mlp_kernel(x_ref, gpre_ref, w1_ref, w2_ref, gpost_ref, o_ref):
    x = x_ref[...]
    h = _rms(x, gpre_ref[...]).astype(BF16)
    y = jnp.zeros(x.shape, F32)
    chunk = 1024
    for c in range(D_FF // chunk):
        a = jnp.maximum(_dot(h, w1_ref[:, c * chunk:(c + 1) * chunk]), 0.0)
        y = y + _dot((a * a).astype(BF16), w2_ref[c * chunk:(c + 1) * chunk, :])
    o_ref[...] = x + _rms(y, gpost_ref[...])


def _mlp(xf, gpre, w1, w2, gpost):
    T = xf.shape[0]
    row = pl.BlockSpec((ROW_TILE, D_MODEL), lambda i: (i, 0))
    full = lambda a: pl.BlockSpec(a.shape, lambda i: (0,) * a.ndim)
    return pl.pallas_call(
        _mlp_kernel,
        grid=(T // ROW_TILE,),
        in_specs=[row, full(gpre), full(w1), full(w2), full(gpost)],
        out_specs=row,
        out_shape=jax.ShapeDtypeStruct((T, D_MODEL), F32),
        compiler_params=_params("parallel"),
        name="mlp",
    )(xf, gpre, w1, w2, gpost)


def kernel(x, mem, rel_bias, mix_pre_g, mix_post_g, w_in, pool_w, pool_scale, conv_w, conv_b, conv_ln_g,
           conv_ln_b, conv_pw, cmp_k_pos, cmp_k_w1, cmp_k_w2, cmp_v_pos, cmp_v_w1, cmp_v_w2, w_out,
           xa_pre_g, xa_post_g, mem_g, xa_wq, xa_wk, xa_wv, xa_wo, mlp_pre_g, mlp_post_g, mlp_w1, mlp_w2):
    B, S, _ = x.shape
    assert S % ROW_TILE == 0 and S // SEL_BLOCK >= N_SELECT and S // SEL_BLOCK <= LANES
    T = B * S
    ncp = S // CMP_STRIDE
    row2 = lambda a: a.reshape(1, -1)

    tw, tc = _bias_tables(rel_bias)
    twt = tw.reshape(N_KV, QROWS, WINDOW + QBLK).transpose(0, 2, 1)
    tct = tc.reshape(N_KV, QROWS, LANES).transpose(0, 2, 1)
    eblk, ovlt, place = _nsa_tables(S)

    pad_keys = (np.arange(S + KTILE)[:, None] >= S) & (np.arange(LANES)[None, :] == HEAD_DIM)
    keys0 = jnp.broadcast_to(jnp.asarray(pad_keys, BF16), (B, N_KV, S + KTILE, LANES))
    vals0 = jnp.zeros((B, N_KV, VROWS, S + KTILE), BF16)

    for l in range(DEPTH):
        w_a, w_bt = _in_proj_weights(w_in[l])
        up, uc, kcr, ks, kw, qt, vst, vwt, gates = _in_proj(x, row2(mix_pre_g[l]), w_a, w_bt,
                                                            keys0, keys0, vals0, vals0)

        pw_bd = jnp.zeros((POOL_WIDTH, POOL_WIDTH), F32)
        for g in range(len(POOL_WINDOWS)):
            pw_bd = lax.dynamic_update_slice(pw_bd, pool_w[l, g], (g * POOL_GROUP, g * POOL_GROUP))
        yp, yc = _mixers(up, uc, pw_bd.astype(BF16), row2(pool_scale[l]), conv_w[l], row2(conv_b[l]),
                         row2(conv_ln_g[l]), row2(conv_ln_b[l]), conv_pw[l].astype(BF16))

        cpos = jnp.stack([cmp_k_pos[l], cmp_v_pos[l]]).reshape(2, 2, 1, CMP_STRIDE * HEAD_DIM)
        cw1 = jnp.stack([cmp_k_w1[l], cmp_v_w1[l]]).astype(BF16)
        kc, vct = _compress(kcr.reshape(B, 2, N_KV, ncp, CMP_STRIDE * HEAD_DIM), cpos, cw1,
                            cmp_k_w2[l].astype(BF16), cmp_v_w2[l].T.astype(BF16))

        ynt = _nsa(qt, kc, vct, eblk, ks, vst, kw, vwt, gates, twt, tct, ovlt, place)
        x = _out_proj(x, yp, yc, ynt, w_out[l].astype(BF16), row2(mix_post_g[l]))

        km, vm = _mem_kv(mem, row2(mem_g[l]), xa_wk[l].astype(BF16), xa_wv[l].astype(BF16))
        x = _xattn(x, row2(xa_pre_g[l]), xa_wq[l].astype(BF16), km, vm, xa_wo[l].astype(BF16),
                   row2(xa_post_g[l]))

        x = _mlp(x.reshape(T, D_MODEL), row2(mlp_pre_g[l]), mlp_w1[l].astype(BF16), mlp_w2[l].astype(BF16),
                 row2(mlp_post_g[l])).reshape(B, S, D_MODEL)
    return x
```

```python
import functools
import math

import numpy as np
import jax
import jax.numpy as jnp
from jax import lax
from jax.experimental import pallas as pl
from jax.experimental.pallas import tpu as pltpu

F32 = jnp.float32
BF16 = jnp.bfloat16

D_MODEL = 1024
DEPTH = 2
POOL_WIDTH = 256
POOL_WINDOWS = (2, 4, 8, 16)
POOL_GROUP = 64
CONV_WIDTH = 256
CONV_KSIZE = 31
NSA_WIDTH = 512
HEAD_DIM = 64
N_HEADS = 8
N_KV = 2
Q_PER_KV = 4
CMP_LEN = 32
CMP_STRIDE = 16
CMP_HIDDEN = 256
SEL_BLOCK = 64
N_SELECT = 16
WINDOW = 512
N_BRANCH = 3
N_BUCKETS = 32
MAX_EXACT = 16
MAX_DISTANCE = 128
XA_HEADS = 4
XA_HEAD_DIM = 256
D_FF = 4096
EPS = 1e-6
NEG_INF = -1e30
FORCE = 1e30

LANES = 128
ROW_TILE = 512
QBLK = 256
KTILE = 256
QROWS = Q_PER_KV * QBLK
HALO = 32
CNEAR = 24
CNEAR_LO = 8
NEG_BF16 = -(2.0 ** 100)
M_INIT = -1e29
TAKEN = -3e38
N_FORCED = 3
LOG2E = 1.4426950408889634
FAR_UNROLL = 2
VROWS = 80
VMEM_LIMIT = 56 * 1024 * 1024


def _rms(x, g):
    return x * lax.rsqrt(jnp.mean(x * x, axis=-1, keepdims=True) + EPS) * g


def _dot(a, b):
    return jnp.dot(a, b, preferred_element_type=F32)


def _dot_nt(a, b):
    return lax.dot_general(a, b, (((1,), (1,)), ((), ())), preferred_element_type=F32)


def _params(*sem):
    return pltpu.CompilerParams(dimension_semantics=sem, vmem_limit_bytes=VMEM_LIMIT)


def _in_proj_weights(w):
    z64 = jnp.zeros((D_MODEL, LANES - HEAD_DIM), w.dtype)
    z4 = jnp.zeros((D_MODEL, 4), w.dtype)
    kv0 = POOL_WIDTH + 2 * CONV_WIDTH + NSA_WIDTH
    col = lambda j, g: w[:, kv0 + j * 128 + g * HEAD_DIM:kv0 + j * 128 + (g + 1) * HEAD_DIM]
    w_a = jnp.concatenate([w[:, 0:768], col(2, 0), z64, col(2, 1), z64, col(4, 0), z64, col(4, 1), z64,
                           w[:, kv0:kv0 + 256]], axis=1)
    gw = w[:, kv0 + 768:]
    w_b = jnp.concatenate([w[:, 768:1280], w[:, kv0 + 384:kv0 + 512], w[:, kv0 + 640:kv0 + 768],
                           gw[:, 0:12], z4, gw[:, 12:24], z4], axis=1)
    return w_a.astype(BF16), w_b.T.astype(BF16)


def _in_proj_kernel(x_ref, g_ref, wa_ref, wbt_ref, ks_in, kw_in, vs_in, vw_in,
                    pool_ref, conv_ref, kc_ref, ks_ref, kw_ref, qt_ref, vs_ref, vw_ref, gate_ref):
    del ks_in, kw_in, vs_in, vw_in
    tm = x_ref.shape[1]
    h = _rms(x_ref[0], g_ref[...]).astype(BF16)
    pool_ref[0] = _dot(h, wa_ref[:, 0:256])
    conv_ref[0] = _dot(h, wa_ref[:, 256:768])
    ksw = _dot(h, wa_ref[:, 768:1280]).astype(BF16)
    kvc = _dot(h, wa_ref[:, 1280:1536])
    for g in range(N_KV):
        ks_ref[0, g] = ksw[:, g * LANES:(g + 1) * LANES]
        kw_ref[0, g] = ksw[:, (N_KV + g) * LANES:(N_KV + g + 1) * LANES]
        for kv in range(2):
            kc_ref[0, kv, g] = kvc[:, (kv * N_KV + g) * HEAD_DIM:(kv * N_KV + g + 1) * HEAD_DIM]

    pt = _dot_nt(wbt_ref[...], h)
    qs = (pt[0:NSA_WIDTH] * (HEAD_DIM ** -0.5 * LOG2E)).astype(BF16)
    ones_row = jnp.where(lax.broadcasted_iota(jnp.int32, (VROWS - HEAD_DIM, tm), 0) == 0, 1.0, 0.0).astype(BF16)
    gs = jax.nn.sigmoid(pt[NSA_WIDTH + 256:NSA_WIDTH + 288])
    for g in range(N_KV):
        for qb in range(tm // QBLK):
            cols = slice(qb * QBLK, (qb + 1) * QBLK)
            qt_ref[0, g, qb] = jnp.concatenate(
                [qs[(g * Q_PER_KV + r) * HEAD_DIM:(g * Q_PER_KV + r + 1) * HEAD_DIM, cols] for r in range(Q_PER_KV)],
                axis=1)
            gate_ref[0, qb, g] = gs[g * 16:g * 16 + Q_PER_KV * N_BRANCH, cols]
        v0 = NSA_WIDTH + g * HEAD_DIM
        vs_ref[0, g] = jnp.concatenate([pt[v0:v0 + HEAD_DIM].astype(BF16), ones_row], axis=0)
        vw_ref[0, g] = jnp.concatenate([pt[v0 + 128:v0 + 128 + HEAD_DIM].astype(BF16), ones_row], axis=0)


def _in_proj(x, g, w_a, w_bt, ks0, kw0, vs0, vw0):
    B, S, _ = x.shape
    tm = ROW_TILE
    nq = tm // QBLK
    full = lambda a: pl.BlockSpec(a.shape, lambda b, s: (0,) * a.ndim)
    anyspec = pl.BlockSpec(memory_space=pl.ANY)
    seq = lambda n: pl.BlockSpec((1, tm, n), lambda b, s: (b, s, 0))
    keys = pl.BlockSpec((1, N_KV, tm, LANES), lambda b, s: (b, 0, s, 0))
    vals = pl.BlockSpec((1, N_KV, VROWS, tm), lambda b, s: (b, 0, 0, s))
    return pl.pallas_call(
        _in_proj_kernel,
        grid=(B, S // tm),
        in_specs=[seq(D_MODEL), full(g), full(w_a), full(w_bt), anyspec, anyspec, anyspec, anyspec],
        out_specs=[
            seq(POOL_WIDTH), seq(2 * CONV_WIDTH),
            pl.BlockSpec((1, 2, N_KV, tm, HEAD_DIM), lambda b, s: (b, 0, 0, s, 0)),
            keys, keys,
            pl.BlockSpec((1, N_KV, nq, HEAD_DIM, QROWS), lambda b, s: (b, 0, s, 0, 0)),
            vals, vals,
            pl.BlockSpec((1, nq, N_KV, Q_PER_KV * N_BRANCH, QBLK), lambda b, s: (b, s, 0, 0, 0)),
        ],
        out_shape=[
            jax.ShapeDtypeStruct((B, S, POOL_WIDTH), F32),
            jax.ShapeDtypeStruct((B, S, 2 * CONV_WIDTH), F32),
            jax.ShapeDtypeStruct((B, 2, N_KV, S, HEAD_DIM), F32),
            jax.ShapeDtypeStruct(ks0.shape, BF16),
            jax.ShapeDtypeStruct(kw0.shape, BF16),
            jax.ShapeDtypeStruct((B, N_KV, S // QBLK, HEAD_DIM, QROWS), BF16),
            jax.ShapeDtypeStruct(vs0.shape, BF16),
            jax.ShapeDtypeStruct(vw0.shape, BF16),
            jax.ShapeDtypeStruct((B, S // QBLK, N_KV, Q_PER_KV * N_BRANCH, QBLK), F32),
        ],
        input_output_aliases={4: 3, 5: 4, 6: 6, 7: 7},
        compiler_params=_params("parallel", "parallel"),
        name="in_proj",
    )(x, g, w_a, w_bt, ks0, kw0, vs0, vw0)


def _mixers_kernel(up_ref, uc_ref, pw_ref, ps_ref, cw_ref, cb_ref, lg_ref, lb_ref, cpw_ref,
                   yp_ref, yc_ref, ubuf, hbuf):
    s = pl.program_id(1)
    ts = up_ref.shape[1]

    @pl.when(s == 0)
    def _():
        ubuf[0:HALO, :] = jnp.zeros((HALO, POOL_WIDTH), F32)
        hbuf[0:HALO, :] = jnp.zeros((HALO, CONV_WIDTH), F32)

    ext = HALO + ts
    back = lambda a, k: a + pltpu.roll(a, k, 0)

    u = up_ref[0]
    ubuf[HALO:ext, :] = u
    xl = ubuf[:, 0:128]
    xh = ubuf[:, 128:256]
    s2l = back(xl, 1)
    s4l = back(s2l, 2)
    s8h = back(back(back(xh, 1), 2), 4)
    s16h = back(s8h, 8)
    t1 = (s * ts + lax.broadcasted_iota(jnp.int32, (ts, 128), 0) + 1).astype(F32)
    lane = lax.broadcasted_iota(jnp.int32, (ts, 128), 1)
    first = lane < POOL_GROUP
    sum_lo = jnp.where(first, s2l[HALO:ext], s4l[HALO:ext])
    den_lo = jnp.minimum(t1, jnp.where(first, 2.0, 4.0))
    sum_hi = jnp.where(first, s8h[HALO:ext], s16h[HALO:ext])
    den_hi = jnp.minimum(t1, jnp.where(first, 8.0, 16.0))
    d = jnp.concatenate([sum_lo / den_lo - u[:, 0:128], sum_hi / den_hi - u[:, 128:256]], axis=1)
    yp_ref[0] = (_dot(d.astype(BF16), pw_ref[...]) * ps_ref[...]).astype(BF16)
    ubuf[0:HALO, :] = ubuf[ts:ext, :]

    uc = uc_ref[0]
    hbuf[HALO:ext, :] = uc[:, 0:CONV_WIDTH] * jax.nn.sigmoid(uc[:, CONV_WIDTH:])
    halves = []
    for c0 in range(0, CONV_WIDTH, LANES):
        hx = hbuf[:, c0:c0 + LANES]
        ahead = [hx] + [pltpu.roll(hx, ext - r, 0) for r in range(1, 8)]
        acc_h = jnp.zeros((ts, LANES), F32) + cb_ref[:, c0:c0 + LANES]
        for j in range(CONV_KSIZE):
            a8, r = divmod(HALO - (CONV_KSIZE - 1) + j, 8)
            acc_h = acc_h + ahead[r][8 * a8:8 * a8 + ts] * cw_ref[j:j + 1, c0:c0 + LANES]
        halves.append(acc_h)
    acc = jnp.concatenate(halves, axis=1)
    mu = jnp.mean(acc, axis=-1, keepdims=True)
    var = jnp.mean(jnp.square(acc - mu), axis=-1, keepdims=True)
    y = (acc - mu) * lax.rsqrt(var + EPS) * lg_ref[...] + lb_ref[...]
    y = y * jax.nn.sigmoid(y)
    yc_ref[0] = _dot(y.astype(BF16), cpw_ref[...]).astype(BF16)
    hbuf[0:HALO, :] = hbuf[ts:ts + HALO, :]


def _mixers(up, uc, pw, ps, cw, cb, lg, lb, cpw):
    B, S, _ = up.shape
    ts = ROW_TILE
    seq = lambda n: pl.BlockSpec((1, ts, n), lambda b, s: (b, s, 0))
    full = lambda a: pl.BlockSpec(a.shape, lambda b, s: (0,) * a.ndim)
    return pl.pallas_call(
        _mixers_kernel,
        grid=(B, S // ts),
        in_specs=[seq(256), seq(512)] + [full(a) for a in (pw, ps, cw, cb, lg, lb, cpw)],
        out_specs=[seq(256), seq(256)],
        out_shape=[jax.ShapeDtypeStruct((B, S, 256), BF16)] * 2,
        scratch_shapes=[pltpu.VMEM((HALO + ts, POOL_WIDTH), F32), pltpu.VMEM((HALO + ts, CONV_WIDTH), F32)],
        compiler_params=_params("parallel", "arbitrary"),
        name="mixers",
    )(up, uc, pw, ps, cw, cb, lg, lb, cpw)


def _compress_kernel(r_ref, pos_ref, w1_ref, w2k_ref, w2vt_ref, kc_ref, vct_ref):
    ncp = r_ref.shape[3] // CMP_STRIDE
    half = CMP_STRIDE * HEAD_DIM

    def hidden(kv):
        r = jnp.concatenate([r_ref[0, kv, 0, pl.ds(l, ncp, stride=CMP_STRIDE), :] for l in range(CMP_STRIDE)],
                            axis=1)
        a = _dot((r + pos_ref[kv, 0]).astype(BF16), w1_ref[kv, 0:half, :])
        b = _dot((r + pos_ref[kv, 1]).astype(BF16), w1_ref[kv, half:2 * half, :])
        return jax.nn.gelu(a + pltpu.roll(b, ncp - 1, 0)).astype(BF16)

    kc = _dot(hidden(0), w2k_ref[...]).astype(BF16)
    kc_ref[0, 0] = jnp.concatenate([kc, jnp.zeros((ncp, LANES - HEAD_DIM), BF16)], axis=1)
    ones_row = jnp.where(lax.broadcasted_iota(jnp.int32, (VROWS - HEAD_DIM, ncp), 0) == 0, 1.0, 0.0).astype(BF16)
    vct_ref[0, 0] = jnp.concatenate([_dot_nt(w2vt_ref[...], hidden(1)).astype(BF16), ones_row], axis=0)


def _compress(r, pos, w1, w2k, w2vt):
    B, _, G, S, _ = r.shape
    ncp = S // CMP_STRIDE
    full = lambda a: pl.BlockSpec(a.shape, lambda b, g: (0,) * a.ndim)
    return pl.pallas_call(
        _compress_kernel,
        grid=(B, G),
        in_specs=[pl.BlockSpec((1, 2, 1, S, HEAD_DIM), lambda b, g: (b, 0, g, 0, 0)),
                  full(pos), full(w1), full(w2k), full(w2vt)],
        out_specs=[pl.BlockSpec((1, 1, ncp, LANES), lambda b, g: (b, g, 0, 0)),
                   pl.BlockSpec((1, 1, VROWS, ncp), lambda b, g: (b, g, 0, 0))],
        out_shape=[jax.ShapeDtypeStruct((B, G, ncp, LANES), BF16),
                   jax.ShapeDtypeStruct((B, G, VROWS, ncp), BF16)],
        compiler_params=_params("parallel", "parallel"),
        name="compress",
    )(r, pos, w1, w2k, w2vt)


def _rel_bucket_np(dist):
    n = np.maximum(dist, 0)
    nf = np.maximum(n, 1).astype(np.float64)
    large = MAX_EXACT + (np.log(nf / MAX_EXACT) / math.log(MAX_DISTANCE / MAX_EXACT)
                         * (N_BUCKETS - MAX_EXACT)).astype(np.int64)
    return np.where(n < MAX_EXACT, n, np.minimum(large, N_BUCKETS - 1)).astype(np.int32)


def _static_maps():
    qq = np.arange(QBLK)[:, None]
    kk = np.arange(WINDOW + QBLK)[None, :]
    dist = qq + WINDOW - kk
    wmap = np.where((dist >= 0) & (dist < WINDOW), _rel_bucket_np(dist), -1)
    u = np.arange(LANES)[None, :]
    dist_c = qq - CMP_STRIDE * (u - CNEAR_LO) - (CMP_LEN - 1)
    cmap = np.where((dist_c >= 0) & (u < CNEAR), _rel_bucket_np(dist_c), -1)
    cmap = np.where(u < CNEAR, cmap, -2)
    return wmap.astype(np.int32), cmap.astype(np.int32)


def _tables_kernel(bias_ref, wmap_ref, cmap_ref, tw_ref, tc_ref):
    h = pl.program_id(0)
    far = bias_ref[N_BUCKETS - 1, h]

    def lookup(m):
        out = jnp.where(m == -1, NEG_INF, 0.0).astype(F32)
        for b in range(N_BUCKETS):
            out = jnp.where(m == b, (bias_ref[b, h] - far) * LOG2E, out)
        return out

    tw_ref[0] = lookup(wmap_ref[...])
    cmap = cmap_ref[...]
    val = lookup(cmap)
    hi = val.astype(BF16)
    lo = (val - hi.astype(F32)).astype(BF16)
    col = lax.broadcasted_iota(jnp.int32, cmap.shape, 1)
    hi_part = jnp.where(col < CNEAR, hi.astype(F32), 0.0)
    lo_part = pltpu.roll(jnp.where(col < CNEAR, lo.astype(F32), 0.0), CNEAR, 1)
    one = jnp.where(col == 2 * CNEAR, 1.0, 0.0)
    tc_ref[0] = (hi_part + lo_part + one).astype(BF16)


def _bias_tables(rel_bias):
    wmap, cmap = _static_maps()
    nw = WINDOW + QBLK
    return pl.pallas_call(
        _tables_kernel,
        grid=(N_HEADS,),
        in_specs=[
            pl.BlockSpec(memory_space=pltpu.SMEM),
            pl.BlockSpec((QBLK, nw), lambda h: (0, 0)),
            pl.BlockSpec((QBLK, LANES), lambda h: (0, 0)),
        ],
        out_specs=[
            pl.BlockSpec((1, QBLK, nw), lambda h: (h, 0, 0)),
            pl.BlockSpec((1, QBLK, LANES), lambda h: (h, 0, 0)),
        ],
        out_shape=[
            jax.ShapeDtypeStruct((N_HEADS, QBLK, nw), F32),
            jax.ShapeDtypeStruct((N_HEADS, QBLK, LANES), BF16),
        ],
        compiler_params=_params("parallel"),
        name="bias_tables",
    )(rel_bias, jnp.asarray(wmap), jnp.asarray(cmap))


def _nsa_tables(S):
    ncp = S // CMP_STRIDE
    nb = LANES
    pos = np.arange(S + KTILE)
    eblk = (pos[:, None] // SEL_BLOCK == np.arange(nb)[None, :]) & (pos[:, None] < S)
    cs = np.arange(ncp)[None, :] * CMP_STRIDE
    ss = np.arange(nb)[:, None] * SEL_BLOCK
    ovlt = (cs < ss + SEL_BLOCK) & (cs + CMP_LEN > ss) & (np.arange(ncp)[None, :] < ncp - 1) & (ss < S)
    rel = np.arange(2 * ncp)[:, None] - ncp + CNEAR_LO
    u = np.arange(LANES)[None, :]
    place = np.where((rel == u) & (u < CNEAR), 1.0, 0.0)
    place = np.where((rel == u - CNEAR) & (u >= CNEAR) & (u < 2 * CNEAR), 1.0, place)
    place = np.where((u == 2 * CNEAR) & (rel >= CNEAR), NEG_BF16, place)
    return jnp.asarray(eblk, BF16), jnp.asarray(ovlt, BF16), jnp.asarray(place, BF16)


def _nsa_kernel(qt_ref, kc_ref, vc_ref, eb_ref, ks_ref, vs_ref, kw_ref, vw_ref, gate_ref, tw_ref, tc_ref,
                ovl_ref, place_ref, o_ref, qaug_ref, s0_ref, s1_ref, m_ref, acc_ref):
    i = pl.program_id(2)
    ncp = kc_ref.shape[2]
    nb = ovl_ref.shape[0]
    pad_tile = ks_ref.shape[2] // KTILE - 1
    col_max = lambda a: jnp.max(a, axis=0, keepdims=True)
    prob = lambda a: jnp.exp2(a).astype(BF16)
    tile_off = lambda t, ok: pl.multiple_of(jnp.where(ok, t, pad_tile) * KTILE, KTILE)

    row = lax.broadcasted_iota(jnp.int32, (LANES - HEAD_DIM, QROWS), 0)
    qn = jnp.concatenate([qt_ref[0, 0, 0], jnp.where(row == 0, NEG_BF16, 0.0).astype(BF16)], axis=0)
    qaug_ref[nb:nb + LANES, :] = qn

    place = place_ref[pl.ds(pl.multiple_of(ncp - CMP_STRIDE * i, CMP_STRIDE), ncp), :]
    s = _dot(jnp.concatenate([kc_ref[0, 0], place], axis=1), jnp.concatenate([qn, tc_ref[0]], axis=0))
    pb = prob(s - jnp.maximum(col_max(s), M_INIT))
    acc_c = _dot(vc_ref[0, 0], pb)
    inv_l = 1.0 / jnp.maximum(acc_c[HEAD_DIM:HEAD_DIM + 1, :], 1e-30)
    o_c = acc_c[0:HEAD_DIM, :] * inv_l
    imp4 = _dot(ovl_ref[...], pb) * inv_l

    win = []
    for tt in range(WINDOW // KTILE + 1):
        t = i - WINDOW // KTILE + tt
        o = tile_off(t, t >= 0)
        win.append((o, _dot(kw_ref[0, 0, pl.ds(o, KTILE), :], qn) + tw_ref[0, tt * KTILE:(tt + 1) * KTILE, :]))
    mw = functools.reduce(jnp.maximum, [col_max(sw) for _, sw in win])
    acc_w = sum(_dot(vw_ref[0, 0, :, pl.ds(o, KTILE)], prob(sw - mw)) for o, sw in win)

    imp = imp4[:, 0:QBLK] + imp4[:, QBLK:2 * QBLK] + imp4[:, 2 * QBLK:3 * QBLK] + imp4[:, 3 * QBLK:4 * QBLK]
    jb = lax.broadcasted_iota(jnp.int32, (nb, QBLK), 0)
    qq = lax.broadcasted_iota(jnp.int32, (nb, QBLK), 1)
    cur = (QBLK // SEL_BLOCK) * i + (qq >> 6)
    forced = (jb == 0) | (jb == cur) | (jb == cur - 1)
    sc = jnp.where(forced, TAKEN, jnp.where(jb <= cur, imp, NEG_INF))
    jio = jb.astype(F32)
    for _ in range(N_SELECT - N_FORCED):
        idx = jnp.min(jnp.where(sc == col_max(sc), jio, float(nb)), axis=0, keepdims=True)
        sc = jnp.where(jio == idx, TAKEN, sc)
    msel = jnp.where(sc == TAKEN, 0.0, NEG_BF16).astype(BF16)
    qaug_ref[0:nb, :] = jnp.concatenate([msel] * Q_PER_KV, axis=1)

    def sel_logits(o):
        keys = jnp.concatenate([eb_ref[pl.ds(o, KTILE), :], ks_ref[0, 0, pl.ds(o, KTILE), :]], axis=1)
        return _dot(keys, qaug_ref[...])

    o_prev = tile_off(i - 1, i >= 1)
    o_diag = pl.multiple_of(i * KTILE, KTILE)
    s_prev = sel_logits(o_prev) + tw_ref[0, WINDOW - KTILE:WINDOW, :]
    s_diag = sel_logits(o_diag) + tw_ref[0, WINDOW:WINDOW + KTILE, :]
    m0 = jnp.maximum(col_max(s_prev), col_max(s_diag))
    acc_ref[...] = (_dot(vs_ref[0, 0, :, pl.ds(o_prev, KTILE)], prob(s_prev - m0))
                    + _dot(vs_ref[0, 0, :, pl.ds(o_diag, KTILE)], prob(s_diag - m0)))
    m_ref[...] = jnp.broadcast_to(m0, m_ref.shape)

    n_far = jnp.maximum(i - 1, 0)
    far_off = lambda t: tile_off(t, t < n_far)

    def update(s_ref, t):
        o = far_off(t)
        sl = s_ref[...]
        m_prev = m_ref[...]
        m_next = jnp.maximum(m_prev, col_max(sl))
        alpha = jnp.exp2(m_prev - m_next)
        pt = prob(sl - m_next[0:1, :])
        acc_ref[...] = alpha[0:1, :] * acc_ref[...] + _dot(vs_ref[0, 0, :, pl.ds(o, KTILE)], pt)
        m_ref[...] = m_next

    s0_ref[...] = sel_logits(far_off(0))

    def far_body(k, carry):
        s1_ref[...] = sel_logits(far_off(2 * k + 1))
        update(s0_ref, 2 * k)
        s0_ref[...] = sel_logits(far_off(2 * k + 2))
        update(s1_ref, 2 * k + 1)
        return carry

    lax.fori_loop(0, (n_far + 1) // 2, far_body, 0)
    acc_s = acc_ref[...]

    gb = gate_ref[0, 0, 0]
    gate = lambda br: jnp.concatenate(
        [gb[r * N_BRANCH + br:r * N_BRANCH + br + 1, :] for r in range(Q_PER_KV)], axis=1)
    out = (gate(0) * o_c
           + (gate(1) / acc_s[HEAD_DIM:HEAD_DIM + 1, :]) * acc_s[0:HEAD_DIM, :]
           + (gate(2) / acc_w[HEAD_DIM:HEAD_DIM + 1, :]) * acc_w[0:HEAD_DIM, :]).astype(BF16)
    for r in range(Q_PER_KV):
        o_ref[0, r * HEAD_DIM:(r + 1) * HEAD_DIM, :] = out[:, r * QBLK:(r + 1) * QBLK]


def _nsa(qt, kc, vct, eblk, ks, vst, kw, vwt, gates, twt, tct, ovlt, place):
    B, G, NQ, _, _ = qt.shape
    S = NQ * QBLK
    nb = ovlt.shape[0]
    per_bg = lambda a: pl.BlockSpec((1, 1) + a.shape[2:], lambda b, g, i: (b, g) + (0,) * (a.ndim - 2))
    per_g = lambda a: pl.BlockSpec((1,) + a.shape[1:], lambda b, g, i: (g,) + (0,) * (a.ndim - 1))
    const = lambda a: pl.BlockSpec(a.shape, lambda b, g, i: (0,) * a.ndim)
    return pl.pallas_call(
        _nsa_kernel,
        grid=(B, G, NQ),
        in_specs=[pl.BlockSpec((1, 1, 1, HEAD_DIM, QROWS), lambda b, g, i: (b, g, i, 0, 0)),
                  per_bg(kc), per_bg(vct), const(eblk), per_bg(ks), per_bg(vst), per_bg(kw), per_bg(vwt),
                  pl.BlockSpec((1, 1, 1, Q_PER_KV * N_BRANCH, QBLK), lambda b, g, i: (b, i, g, 0, 0)),
                  per_g(twt), per_g(tct), const(ovlt), const(place)],
        out_specs=pl.BlockSpec((1, Q_PER_KV * HEAD_DIM, QBLK), lambda b, g, i: (b, g, i)),
        out_shape=jax.ShapeDtypeStruct((B, NSA_WIDTH, S), BF16),
        scratch_shapes=[pltpu.VMEM((nb + LANES, QROWS), BF16),
                        pltpu.VMEM((KTILE, QROWS), F32), pltpu.VMEM((KTILE, QROWS), F32),
                        pltpu.VMEM((8, QROWS), F32), pltpu.VMEM((VROWS, QROWS), F32)],
        compiler_params=_params("parallel", "parallel", "arbitrary"),
        name="nsa",
    )(qt, kc, vct, eblk, ks, vst, kw, vwt, gates, twt, tct, ovlt, place)


def _mem_kv_kernel(mem_ref, g_ref, wk_ref, wv_ref, k_ref, v_ref):
    m = _rms(mem_ref[0], g_ref[...]).astype(BF16)
    k_ref[0] = _dot(m, wk_ref[...]).astype(BF16)
    v_ref[0] = _dot(m, wv_ref[...]).astype(BF16)


def _mem_kv(mem, g, wk, wv):
    B, M, _ = mem.shape
    full = lambda a: pl.BlockSpec(a.shape, lambda b: (0,) * a.ndim)
    per_b = pl.BlockSpec((1, M, D_MODEL), lambda b: (b, 0, 0))
    return pl.pallas_call(
        _mem_kv_kernel,
        grid=(B,),
        in_specs=[per_b, full(g), full(wk), full(wv)],
        out_specs=[per_b, per_b],
        out_shape=[jax.ShapeDtypeStruct((B, M, D_MODEL), BF16)] * 2,
        compiler_params=_params("parallel"),
        name="mem_kv",
    )(mem, g, wk, wv)


def _mix_xattn_kernel(x_ref, yp_ref, yc_ref, ynt_ref, wmix_ref, gmix_ref,
                      gpre_ref, wq_ref, k_ref, v_ref, wo_ref, gpost_ref, o_ref):
    nsa = lax.dot_general(ynt_ref[0], wmix_ref[512:1024, :], (((0,), (0,)), ((), ())),
                          preferred_element_type=F32)
    mix = _dot(yp_ref[0], wmix_ref[0:256, :]) + _dot(yc_ref[0], wmix_ref[256:512, :]) + nsa
    x = x_ref[0] + _rms(mix, gmix_ref[...])
    h = _rms(x, gpre_ref[...]).astype(BF16)
    qx = (_dot(h, wq_ref[...]) * (XA_HEAD_DIM ** -0.5)).astype(BF16)
    outs = []
    for hh in range(XA_HEADS):
        sl = slice(hh * XA_HEAD_DIM, (hh + 1) * XA_HEAD_DIM)
        s = _dot_nt(qx[:, sl], k_ref[0, :, sl])
        p = jnp.exp(s - jnp.max(s, axis=1, keepdims=True))
        p = p * (1.0 / jnp.sum(p, axis=1, keepdims=True))
        outs.append(_dot(p.astype(BF16), v_ref[0, :, sl]).astype(BF16))
    o = jnp.concatenate(outs, axis=1)
    o_ref[0] = x + _rms(_dot(o, wo_ref[...]), gpost_ref[...])


def _mix_xattn(x, yp, yc, ynt, wmix, gmix, gpre, wq, km, vm, wo, gpost):
    B, S, _ = x.shape
    M = km.shape[1]
    ts = ROW_TILE
    full = lambda a: pl.BlockSpec(a.shape, lambda b, s: (0,) * a.ndim)
    seq = lambda n: pl.BlockSpec((1, ts, n), lambda b, s: (b, s, 0))
    per_b = pl.BlockSpec((1, M, D_MODEL), lambda b, s: (b, 0, 0))
    return pl.pallas_call(
        _mix_xattn_kernel,
        grid=(B, S // ts),
        in_specs=[seq(D_MODEL), seq(POOL_WIDTH), seq(CONV_WIDTH),
                  pl.BlockSpec((1, NSA_WIDTH, ts), lambda b, s: (b, 0, s)), full(wmix), full(gmix),
                  full(gpre), full(wq), per_b, per_b, full(wo), full(gpost)],
        out_specs=seq(D_MODEL),
        out_shape=jax.ShapeDtypeStruct((B, S, D_MODEL), F32),
        compiler_params=_params("parallel", "parallel"),
        name="mix_xattn",
    )(x, yp, yc, ynt, wmix, gmix, gpre, wq, km, vm, wo, gpost)


def _mlp_kernel(x_ref, gpre_ref, w1_ref, w2_ref, gpost_ref, o_ref):
    x = x_ref[...]
    h = _rms(x, gpre_ref[...]).astype(BF16)
    y = jnp.zeros(x.shape, F32)
    chunk = 1024
    for c in range(D_FF // chunk):
        a = jnp.maximum(_dot(h, w1_ref[:, c * chunk:(c + 1) * chunk]), 0.0)
        y = y + _dot((a * a).astype(BF16), w2_ref[c * chunk:(c + 1) * chunk, :])
    o_ref[...] = x + _rms(y, gpost_ref[...])


def _mlp(xf, gpre, w1, w2, gpost):
    T = xf.shape[0]
    row = pl.BlockSpec((ROW_TILE, D_MODEL), lambda i: (i, 0))
    full = lambda a: pl.BlockSpec(a.shape, lambda i: (0,) * a.ndim)
    return pl.pallas_call(
        _mlp_kernel,
        grid=(T // ROW_TILE,),
        in_specs=[row, full(gpre), full(w1), full(w2), full(gpost)],
        out_specs=row,
        out_shape=jax.ShapeDtypeStruct((T, D_MODEL), F32),
        compiler_params=_params("parallel"),
        name="mlp",
    )(xf, gpre, w1, w2, gpost)


def kernel(x, mem, rel_bias, mix_pre_g, mix_post_g, w_in, pool_w, pool_scale, conv_w, conv_b, conv_ln_g,
           conv_ln_b, conv_pw, cmp_k_pos, cmp_k_w1, cmp_k_w2, cmp_v_pos, cmp_v_w1, cmp_v_w2, w_out,
           xa_pre_g, xa_post_g, mem_g, xa_wq, xa_wk, xa_wv, xa_wo, mlp_pre_g, mlp_post_g, mlp_w1, mlp_w2):
    B, S, _ = x.shape
    assert S % ROW_TILE == 0 and S // SEL_BLOCK >= N_SELECT and S // SEL_BLOCK <= LANES
    T = B * S
    ncp = S // CMP_STRIDE
    row2 = lambda a: a.reshape(1, -1)

    tw, tc = _bias_tables(rel_bias)
    twt = tw.reshape(N_KV, QROWS, WINDOW + QBLK).transpose(0, 2, 1)
    tct = tc.reshape(N_KV, QROWS, LANES).transpose(0, 2, 1)
    eblk, ovlt, place = _nsa_tables(S)

    pad_keys = (np.arange(S + KTILE)[:, None] >= S) & (np.arange(LANES)[None, :] == HEAD_DIM)
    keys0 = jnp.broadcast_to(jnp.asarray(pad_keys, BF16), (B, N_KV, S + KTILE, LANES))
    vals0 = jnp.zeros((B, N_KV, VROWS, S + KTILE), BF16)

    for l in range(DEPTH):
        w_a, w_bt = _in_proj_weights(w_in[l])
        up, uc, kcr, ks, kw, qt, vst, vwt, gates = _in_proj(x, row2(mix_pre_g[l]), w_a, w_bt,
                                                            keys0, keys0, vals0, vals0)

        pw_bd = jnp.zeros((POOL_WIDTH, POOL_WIDTH), F32)
        for g in range(len(POOL_WINDOWS)):
            pw_bd = lax.dynamic_update_slice(pw_bd, pool_w[l, g], (g * POOL_GROUP, g * POOL_GROUP))
        yp, yc = _mixers(up, uc, pw_bd.astype(BF16), row2(pool_scale[l]), conv_w[l], row2(conv_b[l]),
                         row2(conv_ln_g[l]), row2(conv_ln_b[l]), conv_pw[l].astype(BF16))

        cpos = jnp.stack([cmp_k_pos[l], cmp_v_pos[l]]).reshape(2, 2, 1, CMP_STRIDE * HEAD_DIM)
        cw1 = jnp.stack([cmp_k_w1[l], cmp_v_w1[l]]).astype(BF16)
        kc, vct = _compress(kcr, cpos, cw1, cmp_k_w2[l].astype(BF16), cmp_v_w2[l].T.astype(BF16))

        ynt = _nsa(qt, kc, vct, eblk, ks, vst, kw, vwt, gates, twt, tct, ovlt, place)

        km, vm = _mem_kv(mem, row2(mem_g[l]), xa_wk[l].astype(BF16), xa_wv[l].astype(BF16))
        x = _mix_xattn(x, yp, yc, ynt, w_out[l].astype(BF16), row2(mix_post_g[l]), row2(xa_pre_g[l]),
                       xa_wq[l].astype(BF16), km, vm, xa_wo[l].astype(BF16), row2(xa_post_g[l]))

        x = _mlp(x.reshape(T, D_MODEL), row2(mlp_pre_g[l]), mlp_w1[l].astype(BF16), mlp_w2[l].astype(BF16),
                 row2(mlp_post_g[l])).reshape(B, S, D_MODEL)
    return x
```

```python
import functools
import math

import numpy as np
import jax
import jax.numpy as jnp
from jax import lax
from jax.experimental import pallas as pl
from jax.experimental.pallas import tpu as pltpu

F32 = jnp.float32
BF16 = jnp.bfloat16

D_MODEL = 1024
DEPTH = 2
POOL_WIDTH = 256
POOL_WINDOWS = (2, 4, 8, 16)
POOL_GROUP = 64
CONV_WIDTH = 256
CONV_KSIZE = 31
NSA_WIDTH = 512
HEAD_DIM = 64
N_HEADS = 8
N_KV = 2
Q_PER_KV = 4
CMP_LEN = 32
CMP_STRIDE = 16
CMP_HIDDEN = 256
SEL_BLOCK = 64
N_SELECT = 16
WINDOW = 512
N_BRANCH = 3
N_BUCKETS = 32
MAX_EXACT = 16
MAX_DISTANCE = 128
XA_HEADS = 4
XA_HEAD_DIM = 256
D_FF = 4096
EPS = 1e-6
NEG_INF = -1e30
FORCE = 1e30

LANES = 128
ROW_TILE = 512
QBLK = 256
KTILE = 256
QROWS = Q_PER_KV * QBLK
HALO = 32
CNEAR = 24
CNEAR_LO = 8
NEG_BF16 = -(2.0 ** 100)
M_INIT = -1e29
TAKEN = -3e38
N_FORCED = 3
LOG2E = 1.4426950408889634
FAR_UNROLLS = (8, 4, 2)
VROWS = 80
VMEM_LIMIT = 56 * 1024 * 1024


def _rms(x, g):
    return x * lax.rsqrt(jnp.mean(x * x, axis=-1, keepdims=True) + EPS) * g


def _dot(a, b):
    return jnp.dot(a, b, preferred_element_type=F32)


def _dot_nt(a, b):
    return lax.dot_general(a, b, (((1,), (1,)), ((), ())), preferred_element_type=F32)


def _params(*sem):
    return pltpu.CompilerParams(dimension_semantics=sem, vmem_limit_bytes=VMEM_LIMIT)


def _in_proj_weights(w):
    z64 = jnp.zeros((D_MODEL, LANES - HEAD_DIM), w.dtype)
    z4 = jnp.zeros((D_MODEL, 4), w.dtype)
    kv0 = POOL_WIDTH + 2 * CONV_WIDTH + NSA_WIDTH
    col = lambda j, g: w[:, kv0 + j * 128 + g * HEAD_DIM:kv0 + j * 128 + (g + 1) * HEAD_DIM]
    w_a = jnp.concatenate([w[:, 0:768], col(2, 0), z64, col(2, 1), z64, col(4, 0), z64, col(4, 1), z64,
                           w[:, kv0:kv0 + 256]], axis=1)
    gw = w[:, kv0 + 768:]
    w_b = jnp.concatenate([w[:, 768:1280], w[:, kv0 + 384:kv0 + 512], w[:, kv0 + 640:kv0 + 768],
                           gw[:, 0:12], z4, gw[:, 12:24], z4], axis=1)
    return w_a.astype(BF16), w_b.T.astype(BF16)


def _in_proj_kernel(x_ref, g_ref, wa_ref, wbt_ref, ks_in, kw_in, vs_in, vw_in,
                    pool_ref, conv_ref, kc_ref, ks_ref, kw_ref, qt_ref, vs_ref, vw_ref, gate_ref):
    del ks_in, kw_in, vs_in, vw_in
    tm = x_ref.shape[1]
    h = _rms(x_ref[0], g_ref[...]).astype(BF16)
    pool_ref[0] = _dot(h, wa_ref[:, 0:256])
    conv_ref[0] = _dot(h, wa_ref[:, 256:768])
    ksw = _dot(h, wa_ref[:, 768:1280]).astype(BF16)
    kvc = _dot(h, wa_ref[:, 1280:1536])
    for g in range(N_KV):
        ks_ref[0, g] = ksw[:, g * LANES:(g + 1) * LANES]
        kw_ref[0, g] = ksw[:, (N_KV + g) * LANES:(N_KV + g + 1) * LANES]
        for kv in range(2):
            kc_ref[0, kv, g] = kvc[:, (kv * N_KV + g) * HEAD_DIM:(kv * N_KV + g + 1) * HEAD_DIM]

    pt = _dot_nt(wbt_ref[...], h)
    qs = (pt[0:NSA_WIDTH] * (HEAD_DIM ** -0.5 * LOG2E)).astype(BF16)
    ones_row = jnp.where(lax.broadcasted_iota(jnp.int32, (VROWS - HEAD_DIM, tm), 0) == 0, 1.0, 0.0).astype(BF16)
    gs = jax.nn.sigmoid(pt[NSA_WIDTH + 256:NSA_WIDTH + 288])
    for g in range(N_KV):
        for qb in range(tm // QBLK):
            cols = slice(qb * QBLK, (qb + 1) * QBLK)
            qt_ref[0, g, qb] = jnp.concatenate(
                [qs[(g * Q_PER_KV + r) * HEAD_DIM:(g * Q_PER_KV + r + 1) * HEAD_DIM, cols] for r in range(Q_PER_KV)],
                axis=1)
            gate_ref[0, qb, g] = gs[g * 16:g * 16 + Q_PER_KV * N_BRANCH, cols]
        v0 = NSA_WIDTH + g * HEAD_DIM
        vs_ref[0, g] = jnp.concatenate([pt[v0:v0 + HEAD_DIM].astype(BF16), ones_row], axis=0)
        vw_ref[0, g] = jnp.concatenate([pt[v0 + 128:v0 + 128 + HEAD_DIM].astype(BF16), ones_row], axis=0)


def _in_proj(x, g, w_a, w_bt, ks0, kw0, vs0, vw0):
    B, S, _ = x.shape
    tm = ROW_TILE
    nq = tm // QBLK
    full = lambda a: pl.BlockSpec(a.shape, lambda b, s: (0,) * a.ndim)
    anyspec = pl.BlockSpec(memory_space=pl.ANY)
    seq = lambda n: pl.BlockSpec((1, tm, n), lambda b, s: (b, s, 0))
    keys = pl.BlockSpec((1, N_KV, tm, LANES), lambda b, s: (b, 0, s, 0))
    vals = pl.BlockSpec((1, N_KV, VROWS, tm), lambda b, s: (b, 0, 0, s))
    return pl.pallas_call(
        _in_proj_kernel,
        grid=(B, S // tm),
        in_specs=[seq(D_MODEL), full(g), full(w_a), full(w_bt), anyspec, anyspec, anyspec, anyspec],
        out_specs=[
            seq(POOL_WIDTH), seq(2 * CONV_WIDTH),
            pl.BlockSpec((1, 2, N_KV, tm, HEAD_DIM), lambda b, s: (b, 0, 0, s, 0)),
            keys, keys,
            pl.BlockSpec((1, N_KV, nq, HEAD_DIM, QROWS), lambda b, s: (b, 0, s, 0, 0)),
            vals, vals,
            pl.BlockSpec((1, nq, N_KV, Q_PER_KV * N_BRANCH, QBLK), lambda b, s: (b, s, 0, 0, 0)),
        ],
        out_shape=[
            jax.ShapeDtypeStruct((B, S, POOL_WIDTH), F32),
            jax.ShapeDtypeStruct((B, S, 2 * CONV_WIDTH), F32),
            jax.ShapeDtypeStruct((B, 2, N_KV, S, HEAD_DIM), F32),
            jax.ShapeDtypeStruct(ks0.shape, BF16),
            jax.ShapeDtypeStruct(kw0.shape, BF16),
            jax.ShapeDtypeStruct((B, N_KV, S // QBLK, HEAD_DIM, QROWS), BF16),
            jax.ShapeDtypeStruct(vs0.shape, BF16),
            jax.ShapeDtypeStruct(vw0.shape, BF16),
            jax.ShapeDtypeStruct((B, S // QBLK, N_KV, Q_PER_KV * N_BRANCH, QBLK), F32),
        ],
        input_output_aliases={4: 3, 5: 4, 6: 6, 7: 7},
        compiler_params=_params("parallel", "parallel"),
        name="in_proj",
    )(x, g, w_a, w_bt, ks0, kw0, vs0, vw0)


def _mixers_kernel(up_ref, uc_ref, pw_ref, ps_ref, cw_ref, cb_ref, lg_ref, lb_ref, cpw_ref,
                   yp_ref, yc_ref, ubuf, hbuf):
    s = pl.program_id(1)
    ts = up_ref.shape[1]

    @pl.when(s == 0)
    def _():
        ubuf[0:HALO, :] = jnp.zeros((HALO, POOL_WIDTH), F32)
        hbuf[0:HALO, :] = jnp.zeros((HALO, CONV_WIDTH), F32)

    ext = HALO + ts
    back = lambda a, k: a + pltpu.roll(a, k, 0)

    u = up_ref[0]
    ubuf[HALO:ext, :] = u
    xl = ubuf[:, 0:128]
    xh = ubuf[:, 128:256]
    s2l = back(xl, 1)
    s4l = back(s2l, 2)
    s8h = back(back(back(xh, 1), 2), 4)
    s16h = back(s8h, 8)
    t1 = (s * ts + lax.broadcasted_iota(jnp.int32, (ts, 128), 0) + 1).astype(F32)
    lane = lax.broadcasted_iota(jnp.int32, (ts, 128), 1)
    first = lane < POOL_GROUP
    sum_lo = jnp.where(first, s2l[HALO:ext], s4l[HALO:ext])
    den_lo = jnp.minimum(t1, jnp.where(first, 2.0, 4.0))
    sum_hi = jnp.where(first, s8h[HALO:ext], s16h[HALO:ext])
    den_hi = jnp.minimum(t1, jnp.where(first, 8.0, 16.0))
    d = jnp.concatenate([sum_lo / den_lo - u[:, 0:128], sum_hi / den_hi - u[:, 128:256]], axis=1)
    yp_ref[0] = (_dot(d.astype(BF16), pw_ref[...]) * ps_ref[...]).astype(BF16)
    ubuf[0:HALO, :] = ubuf[ts:ext, :]

    uc = uc_ref[0]
    hbuf[HALO:ext, :] = uc[:, 0:CONV_WIDTH] * jax.nn.sigmoid(uc[:, CONV_WIDTH:])
    halves = []
    for c0 in range(0, CONV_WIDTH, LANES):
        hx = hbuf[:, c0:c0 + LANES]
        ahead = [hx] + [pltpu.roll(hx, ext - r, 0) for r in range(1, 8)]
        acc_h = jnp.zeros((ts, LANES), F32) + cb_ref[:, c0:c0 + LANES]
        for j in range(CONV_KSIZE):
            a8, r = divmod(HALO - (CONV_KSIZE - 1) + j, 8)
            acc_h = acc_h + ahead[r][8 * a8:8 * a8 + ts] * cw_ref[j:j + 1, c0:c0 + LANES]
        halves.append(acc_h)
    acc = jnp.concatenate(halves, axis=1)
    mu = jnp.mean(acc, axis=-1, keepdims=True)
    var = jnp.mean(jnp.square(acc - mu), axis=-1, keepdims=True)
    y = (acc - mu) * lax.rsqrt(var + EPS) * lg_ref[...] + lb_ref[...]
    y = y * jax.nn.sigmoid(y)
    yc_ref[0] = _dot(y.astype(BF16), cpw_ref[...]).astype(BF16)
    hbuf[0:HALO, :] = hbuf[ts:ts + HALO, :]


def _mixers(up, uc, pw, ps, cw, cb, lg, lb, cpw):
    B, S, _ = up.shape
    ts = ROW_TILE
    seq = lambda n: pl.BlockSpec((1, ts, n), lambda b, s: (b, s, 0))
    full = lambda a: pl.BlockSpec(a.shape, lambda b, s: (0,) * a.ndim)
    return pl.pallas_call(
        _mixers_kernel,
        grid=(B, S // ts),
        in_specs=[seq(256), seq(512)] + [full(a) for a in (pw, ps, cw, cb, lg, lb, cpw)],
        out_specs=[seq(256), seq(256)],
        out_shape=[jax.ShapeDtypeStruct((B, S, 256), BF16)] * 2,
        scratch_shapes=[pltpu.VMEM((HALO + ts, POOL_WIDTH), F32), pltpu.VMEM((HALO + ts, CONV_WIDTH), F32)],
        compiler_params=_params("parallel", "arbitrary"),
        name="mixers",
    )(up, uc, pw, ps, cw, cb, lg, lb, cpw)


def _compress_kernel(r_ref, pos_ref, w1_ref, w2k_ref, w2vt_ref, kc_ref, vct_ref):
    ncp = r_ref.shape[3] // CMP_STRIDE
    half = CMP_STRIDE * HEAD_DIM

    def hidden(kv):
        r = jnp.concatenate([r_ref[0, kv, 0, pl.ds(l, ncp, stride=CMP_STRIDE), :] for l in range(CMP_STRIDE)],
                            axis=1)
        a = _dot((r + pos_ref[kv, 0]).astype(BF16), w1_ref[kv, 0:half, :])
        b = _dot((r + pos_ref[kv, 1]).astype(BF16), w1_ref[kv, half:2 * half, :])
        return jax.nn.gelu(a + pltpu.roll(b, ncp - 1, 0)).astype(BF16)

    kc = _dot(hidden(0), w2k_ref[...]).astype(BF16)
    kc_ref[0, 0] = jnp.concatenate([kc, jnp.zeros((ncp, LANES - HEAD_DIM), BF16)], axis=1)
    ones_row = jnp.where(lax.broadcasted_iota(jnp.int32, (VROWS - HEAD_DIM, ncp), 0) == 0, 1.0, 0.0).astype(BF16)
    vct_ref[0, 0] = jnp.concatenate([_dot_nt(w2vt_ref[...], hidden(1)).astype(BF16), ones_row], axis=0)


def _compress(r, pos, w1, w2k, w2vt):
    B, _, G, S, _ = r.shape
    ncp = S // CMP_STRIDE
    full = lambda a: pl.BlockSpec(a.shape, lambda b, g: (0,) * a.ndim)
    return pl.pallas_call(
        _compress_kernel,
        grid=(B, G),
        in_specs=[pl.BlockSpec((1, 2, 1, S, HEAD_DIM), lambda b, g: (b, 0, g, 0, 0)),
                  full(pos), full(w1), full(w2k), full(w2vt)],
        out_specs=[pl.BlockSpec((1, 1, ncp, LANES), lambda b, g: (b, g, 0, 0)),
                   pl.BlockSpec((1, 1, VROWS, ncp), lambda b, g: (b, g, 0, 0))],
        out_shape=[jax.ShapeDtypeStruct((B, G, ncp, LANES), BF16),
                   jax.ShapeDtypeStruct((B, G, VROWS, ncp), BF16)],
        compiler_params=_params("parallel", "parallel"),
        name="compress",
    )(r, pos, w1, w2k, w2vt)


def _rel_bucket_np(dist):
    n = np.maximum(dist, 0)
    nf = np.maximum(n, 1).astype(np.float64)
    large = MAX_EXACT + (np.log(nf / MAX_EXACT) / math.log(MAX_DISTANCE / MAX_EXACT)
                         * (N_BUCKETS - MAX_EXACT)).astype(np.int64)
    return np.where(n < MAX_EXACT, n, np.minimum(large, N_BUCKETS - 1)).astype(np.int32)


def _static_maps():
    qq = np.arange(QBLK)[:, None]
    kk = np.arange(WINDOW + QBLK)[None, :]
    dist = qq + WINDOW - kk
    wmap = np.where((dist >= 0) & (dist < WINDOW), _rel_bucket_np(dist), -1)
    u = np.arange(LANES)[None, :]
    dist_c = qq - CMP_STRIDE * (u - CNEAR_LO) - (CMP_LEN - 1)
    cmap = np.where((dist_c >= 0) & (u < CNEAR), _rel_bucket_np(dist_c), -1)
    cmap = np.where(u < CNEAR, cmap, -2)
    return wmap.astype(np.int32), cmap.astype(np.int32)


def _tables_kernel(bias_ref, wmap_ref, cmap_ref, tw_ref, tc_ref):
    h = pl.program_id(0)
    far = bias_ref[N_BUCKETS - 1, h]

    def lookup(m):
        out = jnp.where(m == -1, NEG_INF, 0.0).astype(F32)
        for b in range(N_BUCKETS):
            out = jnp.where(m == b, (bias_ref[b, h] - far) * LOG2E, out)
        return out

    tw_ref[0] = lookup(wmap_ref[...])
    cmap = cmap_ref[...]
    val = lookup(cmap)
    hi = val.astype(BF16)
    lo = (val - hi.astype(F32)).astype(BF16)
    col = lax.broadcasted_iota(jnp.int32, cmap.shape, 1)
    hi_part = jnp.where(col < CNEAR, hi.astype(F32), 0.0)
    lo_part = pltpu.roll(jnp.where(col < CNEAR, lo.astype(F32), 0.0), CNEAR, 1)
    one = jnp.where(col == 2 * CNEAR, 1.0, 0.0)
    tc_ref[0] = (hi_part + lo_part + one).astype(BF16)


def _bias_tables(rel_bias):
    wmap, cmap = _static_maps()
    nw = WINDOW + QBLK
    return pl.pallas_call(
        _tables_kernel,
        grid=(N_HEADS,),
        in_specs=[
            pl.BlockSpec(memory_space=pltpu.SMEM),
            pl.BlockSpec((QBLK, nw), lambda h: (0, 0)),
            pl.BlockSpec((QBLK, LANES), lambda h: (0, 0)),
        ],
        out_specs=[
            pl.BlockSpec((1, QBLK, nw), lambda h: (h, 0, 0)),
            pl.BlockSpec((1, QBLK, LANES), lambda h: (h, 0, 0)),
        ],
        out_shape=[
            jax.ShapeDtypeStruct((N_HEADS, QBLK, nw), F32),
            jax.ShapeDtypeStruct((N_HEADS, QBLK, LANES), BF16),
        ],
        compiler_params=_params("parallel"),
        name="bias_tables",
    )(rel_bias, jnp.asarray(wmap), jnp.asarray(cmap))


def _nsa_tables(S):
    ncp = S // CMP_STRIDE
    nb = LANES
    pos = np.arange(S + KTILE)
    eblk = (pos[:, None] // SEL_BLOCK == np.arange(nb)[None, :]) & (pos[:, None] < S)
    cs = np.arange(ncp)[None, :] * CMP_STRIDE
    ss = np.arange(nb)[:, None] * SEL_BLOCK
    ovlt = (cs < ss + SEL_BLOCK) & (cs + CMP_LEN > ss) & (np.arange(ncp)[None, :] < ncp - 1) & (ss < S)
    rel = np.arange(2 * ncp)[:, None] - ncp + CNEAR_LO
    u = np.arange(LANES)[None, :]
    place = np.where((rel == u) & (u < CNEAR), 1.0, 0.0)
    place = np.where((rel == u - CNEAR) & (u >= CNEAR) & (u < 2 * CNEAR), 1.0, place)
    place = np.where((u == 2 * CNEAR) & (rel >= CNEAR), NEG_BF16, place)
    return jnp.asarray(eblk, BF16), jnp.asarray(ovlt, BF16), jnp.asarray(place, BF16)


def _nsa_kernel(qt_ref, qtn_ref, kc_ref, vc_ref, eb_ref, ks_ref, vs_ref, kw_ref, vw_ref, gate_ref, tw_ref, tc_ref,
                ovl_ref, place_ref, o_ref, qaug_ref, s0_ref, s1_ref, m_ref, acc_ref, oc_next_ref, msel_next_ref):
    i = pl.program_id(2)
    ncp = kc_ref.shape[2]
    nb = ovl_ref.shape[0]
    pad_tile = ks_ref.shape[2] // KTILE - 1
    col_max = lambda a: jnp.max(a, axis=0, keepdims=True)
    prob = lambda a: jnp.exp2(a).astype(BF16)
    tile_off = lambda t, ok: pl.multiple_of(jnp.where(ok, t, pad_tile) * KTILE, KTILE)

    def query_operand(q_blk):
        row = lax.broadcasted_iota(jnp.int32, (LANES - HEAD_DIM, QROWS), 0)
        return jnp.concatenate([q_blk, jnp.where(row == 0, NEG_BF16, 0.0).astype(BF16)], axis=0)

    def select_operands(q_blk, blk):
        place = place_ref[pl.ds(pl.multiple_of(ncp - CMP_STRIDE * blk, CMP_STRIDE), ncp), :]
        return place, jnp.concatenate([query_operand(q_blk), tc_ref[0]], axis=0)

    def cmp_logits(place, rhs, c0, rows):
        return _dot(jnp.concatenate([kc_ref[0, 0, c0:c0 + rows, :], place[c0:c0 + rows]], axis=1), rhs)

    def cmp_values(c0, rows):
        return jnp.concatenate([vc_ref[0, 0, :, c0:c0 + rows], ovl_ref[:, c0:c0 + rows]], axis=0)

    def stage(state, s, vt):
        if state is None:
            m = jnp.maximum(col_max(s), M_INIT)
            return m, _dot(vt, prob(s - m))
        m_prev, acc = state
        m = jnp.maximum(m_prev, col_max(s))
        return m, jnp.exp2(m_prev - m) * acc + _dot(vt, prob(s - m))

    def select_store(acc, blk):
        inv_l = 1.0 / jnp.maximum(acc[HEAD_DIM:HEAD_DIM + 1, :], 1e-30)
        imp4 = acc[VROWS:VROWS + nb, :] * inv_l
        imp = imp4[:, 0:QBLK] + imp4[:, QBLK:2 * QBLK] + imp4[:, 2 * QBLK:3 * QBLK] + imp4[:, 3 * QBLK:4 * QBLK]
        jb = lax.broadcasted_iota(jnp.int32, (nb, QBLK), 0)
        qq = lax.broadcasted_iota(jnp.int32, (nb, QBLK), 1)
        cur = (QBLK // SEL_BLOCK) * blk + (qq >> 6)
        forced = (jb == 0) | (jb == cur) | (jb == cur - 1)
        sc = jnp.where(forced, TAKEN, jnp.where(jb <= cur, imp, NEG_INF))
        jio = jb.astype(F32)
        for _ in range(N_SELECT - N_FORCED):
            idx = jnp.min(jnp.where(sc == col_max(sc), jio, float(nb)), axis=0, keepdims=True)
            sc = jnp.where(jio == idx, TAKEN, sc)
        msel = jnp.where(sc == TAKEN, 0.0, NEG_BF16).astype(BF16)
        oc_next_ref[...] = acc[0:HEAD_DIM, :] * inv_l
        msel_next_ref[...] = jnp.concatenate([msel] * Q_PER_KV, axis=1)

    @pl.when(i == 0)
    def _():
        place, rhs = select_operands(qt_ref[0, 0, 0], 0)
        select_store(stage(None, cmp_logits(place, rhs, 0, ncp), cmp_values(0, ncp))[1], 0)

    qn = query_operand(qt_ref[0, 0, 0])
    qaug_ref[nb:nb + LANES, :] = qn
    qaug_ref[0:nb, :] = msel_next_ref[...]
    o_c = oc_next_ref[...]

    def sel_logits(o):
        keys = jnp.concatenate([eb_ref[pl.ds(o, KTILE), :], ks_ref[0, 0, pl.ds(o, KTILE), :]], axis=1)
        return _dot(keys, qaug_ref[...])

    n_far = jnp.maximum(i - 1, 0)
    far_off = lambda t: tile_off(t, t < n_far)

    nxt = jnp.minimum(i + 1, pl.num_programs(2) - 1)
    place_n, rhs_n = select_operands(qtn_ref[0, 0, 0], nxt)
    ct = min(KTILE, ncp)
    jobs = [("cmp", ct, functools.partial(cmp_logits, place_n, rhs_n, c0, ct), functools.partial(cmp_values, c0, ct))
            for c0 in range(0, ncp, ct)]
    for tt in range(WINDOW // KTILE + 1):
        t = i - WINDOW // KTILE + tt
        o = tile_off(t, t >= 0)
        jobs.append(("win", KTILE,
                     lambda o=o, tt=tt: (_dot(kw_ref[0, 0, pl.ds(o, KTILE), :], qn)
                                         + tw_ref[0, tt * KTILE:(tt + 1) * KTILE, :]),
                     lambda o=o: vw_ref[0, 0, :, pl.ds(o, KTILE)]))
    for o, t0 in ((tile_off(i - 1, i >= 1), WINDOW - KTILE), (pl.multiple_of(i * KTILE, KTILE), WINDOW)):
        jobs.append(("sel", KTILE, lambda o=o, t0=t0: sel_logits(o) + tw_ref[0, t0:t0 + KTILE, :],
                     lambda o=o: vs_ref[0, 0, :, pl.ds(o, KTILE)]))
    jobs.append(("far", KTILE, lambda: sel_logits(far_off(0)), None))

    bufs = (s0_ref, s1_ref)
    bufs[0][0:jobs[0][1], :] = jobs[0][2]()
    state = {}
    for k, (branch, rows, _, values) in enumerate(jobs[:-1]):
        bufs[(k + 1) % 2][0:jobs[k + 1][1], :] = jobs[k + 1][2]()
        state[branch] = stage(state.get(branch), bufs[k % 2][0:rows, :], values())
    far_bufs = (bufs[(len(jobs) - 1) % 2], bufs[len(jobs) % 2])
    acc_w = state["win"][1]
    m_ref[...] = jnp.broadcast_to(state["sel"][0], m_ref.shape)
    acc_ref[...] = state["sel"][1]
    select_store(state["cmp"][1], nxt)

    def update(s_ref, t):
        o = far_off(t)
        sl = s_ref[...]
        m_prev = m_ref[...]
        m_next = jnp.maximum(m_prev, col_max(sl))
        alpha = jnp.exp2(m_prev - m_next)
        pt = prob(sl - m_next[0:1, :])
        acc_ref[...] = alpha[0:1, :] * acc_ref[...] + _dot(vs_ref[0, 0, :, pl.ds(o, KTILE)], pt)
        m_ref[...] = m_next

    def far_trips(first_tile, n_trips, unroll):
        def body(k, carry):
            for h in range(unroll):
                t = first_tile + unroll * k + h
                far_bufs[(h + 1) % 2][...] = sel_logits(far_off(t + 1))
                update(far_bufs[h % 2], t)
            return carry

        lax.fori_loop(0, n_trips, body, 0)

    done = 0
    for unroll in FAR_UNROLLS[:-1]:
        n_trips = (n_far - done) // unroll
        far_trips(done, n_trips, unroll)
        done = done + n_trips * unroll
    last = FAR_UNROLLS[-1]
    far_trips(done, (n_far - done + last - 1) // last, last)
    acc_s = acc_ref[...]

    gb = gate_ref[0, 0, 0]
    gate = lambda br: jnp.concatenate(
        [gb[r * N_BRANCH + br:r * N_BRANCH + br + 1, :] for r in range(Q_PER_KV)], axis=1)
    out = (gate(0) * o_c
           + (gate(1) / acc_s[HEAD_DIM:HEAD_DIM + 1, :]) * acc_s[0:HEAD_DIM, :]
           + (gate(2) / acc_w[HEAD_DIM:HEAD_DIM + 1, :]) * acc_w[0:HEAD_DIM, :]).astype(BF16)
    for r in range(Q_PER_KV):
        o_ref[0, r * HEAD_DIM:(r + 1) * HEAD_DIM, :] = out[:, r * QBLK:(r + 1) * QBLK]


def _nsa(qt, kc, vct, eblk, ks, vst, kw, vwt, gates, twt, tct, ovlt, place):
    B, G, NQ, _, _ = qt.shape
    S = NQ * QBLK
    nb = ovlt.shape[0]
    per_bg = lambda a: pl.BlockSpec((1, 1) + a.shape[2:], lambda b, g, i: (b, g) + (0,) * (a.ndim - 2))
    per_g = lambda a: pl.BlockSpec((1,) + a.shape[1:], lambda b, g, i: (g,) + (0,) * (a.ndim - 1))
    const = lambda a: pl.BlockSpec(a.shape, lambda b, g, i: (0,) * a.ndim)
    return pl.pallas_call(
        _nsa_kernel,
        grid=(B, G, NQ),
        in_specs=[pl.BlockSpec((1, 1, 1, HEAD_DIM, QROWS), lambda b, g, i: (b, g, i, 0, 0)),
                  pl.BlockSpec((1, 1, 1, HEAD_DIM, QROWS), lambda b, g, i: (b, g, jnp.minimum(i + 1, NQ - 1), 0, 0)),
                  per_bg(kc), per_bg(vct), const(eblk), per_bg(ks), per_bg(vst), per_bg(kw), per_bg(vwt),
                  pl.BlockSpec((1, 1, 1, Q_PER_KV * N_BRANCH, QBLK), lambda b, g, i: (b, i, g, 0, 0)),
                  per_g(twt), per_g(tct), const(ovlt), const(place)],
        out_specs=pl.BlockSpec((1, Q_PER_KV * HEAD_DIM, QBLK), lambda b, g, i: (b, g, i)),
        out_shape=jax.ShapeDtypeStruct((B, NSA_WIDTH, S), BF16),
        scratch_shapes=[pltpu.VMEM((nb + LANES, QROWS), BF16),
                        pltpu.VMEM((KTILE, QROWS), F32), pltpu.VMEM((KTILE, QROWS), F32),
                        pltpu.VMEM((8, QROWS), F32), pltpu.VMEM((VROWS, QROWS), F32),
                        pltpu.VMEM((HEAD_DIM, QROWS), F32), pltpu.VMEM((nb, QROWS), BF16)],
        compiler_params=_params("parallel", "parallel", "arbitrary"),
        name="nsa",
    )(qt, qt, kc, vct, eblk, ks, vst, kw, vwt, gates, twt, tct, ovlt, place)


def _mem_kv_kernel(mem_ref, g_ref, wk_ref, wv_ref, k_ref, v_ref):
    m = _rms(mem_ref[0], g_ref[...]).astype(BF16)
    k_ref[0] = _dot(m, wk_ref[...]).astype(BF16)
    v_ref[0] = _dot(m, wv_ref[...]).astype(BF16)


def _mem_kv(mem, g, wk, wv):
    B, M, _ = mem.shape
    full = lambda a: pl.BlockSpec(a.shape, lambda b: (0,) * a.ndim)
    per_b = pl.BlockSpec((1, M, D_MODEL), lambda b: (b, 0, 0))
    return pl.pallas_call(
        _mem_kv_kernel,
        grid=(B,),
        in_specs=[per_b, full(g), full(wk), full(wv)],
        out_specs=[per_b, per_b],
        out_shape=[jax.ShapeDtypeStruct((B, M, D_MODEL), BF16)] * 2,
        compiler_params=_params("parallel"),
        name="mem_kv",
    )(mem, g, wk, wv)


def _mix_xattn_kernel(x_ref, yp_ref, yc_ref, ynt_ref, wmix_ref, gmix_ref,
                      gpre_ref, wq_ref, k_ref, v_ref, wo_ref, gpost_ref, o_ref):
    nsa = lax.dot_general(ynt_ref[0], wmix_ref[512:1024, :], (((0,), (0,)), ((), ())),
                          preferred_element_type=F32)
    mix = _dot(yp_ref[0], wmix_ref[0:256, :]) + _dot(yc_ref[0], wmix_ref[256:512, :]) + nsa
    x = x_ref[0] + _rms(mix, gmix_ref[...])
    h = _rms(x, gpre_ref[...]).astype(BF16)
    qx = (_dot(h, wq_ref[...]) * (XA_HEAD_DIM ** -0.5)).astype(BF16)
    outs = []
    for hh in range(XA_HEADS):
        sl = slice(hh * XA_HEAD_DIM, (hh + 1) * XA_HEAD_DIM)
        s = _dot_nt(qx[:, sl], k_ref[0, :, sl])
        p = jnp.exp(s - jnp.max(s, axis=1, keepdims=True))
        p = p * (1.0 / jnp.sum(p, axis=1, keepdims=True))
        outs.append(_dot(p.astype(BF16), v_ref[0, :, sl]).astype(BF16))
    o = jnp.concatenate(outs, axis=1)
    o_ref[0] = x + _rms(_dot(o, wo_ref[...]), gpost_ref[...])


def _mix_xattn(x, yp, yc, ynt, wmix, gmix, gpre, wq, km, vm, wo, gpost):
    B, S, _ = x.shape
    M = km.shape[1]
    ts = ROW_TILE
    full = lambda a: pl.BlockSpec(a.shape, lambda b, s: (0,) * a.ndim)
    seq = lambda n: pl.BlockSpec((1, ts, n), lambda b, s: (b, s, 0))
    per_b = pl.BlockSpec((1, M, D_MODEL), lambda b, s: (b, 0, 0))
    return pl.pallas_call(
        _mix_xattn_kernel,
        grid=(B, S // ts),
        in_specs=[seq(D_MODEL), seq(POOL_WIDTH), seq(CONV_WIDTH),
                  pl.BlockSpec((1, NSA_WIDTH, ts), lambda b, s: (b, 0, s)), full(wmix), full(gmix),
                  full(gpre), full(wq), per_b, per_b, full(wo), full(gpost)],
        out_specs=seq(D_MODEL),
        out_shape=jax.ShapeDtypeStruct((B, S, D_MODEL), F32),
        compiler_params=_params("parallel", "parallel"),
        name="mix_xattn",
    )(x, yp, yc, ynt, wmix, gmix, gpre, wq, km, vm, wo, gpost)


def _mlp_kernel(x_ref, gpre_ref, w1_ref, w2_ref, gpost_ref, o_ref):
    x = x_ref[...]
    h = _rms(x, gpre_ref[...]).astype(BF16)
    y = jnp.zeros(x.shape, F32)
    chunk = 1024
    for c in range(D_FF // chunk):
        a = jnp.maximum(_dot(h, w1_ref[:, c * chunk:(c + 1) * chunk]), 0.0)
        y = y + _dot((a * a).astype(BF16), w2_ref[c * chunk:(c + 1) * chunk, :])
    o_ref[...] = x + _rms(y, gpost_ref[...])


def _mlp(xf, gpre, w1, w2, gpost):
    T = xf.shape[0]
    row = pl.BlockSpec((ROW_TILE, D_MODEL), lambda i: (i, 0))
    full = lambda a: pl.BlockSpec(a.shape, lambda i: (0,) * a.ndim)
    return pl.pallas_call(
        _mlp_kernel,
        grid=(T // ROW_TILE,),
        in_specs=[row, full(gpre), full(w1), full(w2), full(gpost)],
        out_specs=row,
        out_shape=jax.ShapeDtypeStruct((T, D_MODEL), F32),
        compiler_params=_params("parallel"),
        name="mlp",
    )(xf, gpre, w1, w2, gpost)


def kernel(x, mem, rel_bias, mix_pre_g, mix_post_g, w_in, pool_w, pool_scale, conv_w, conv_b, conv_ln_g,
           conv_ln_b, conv_pw, cmp_k_pos, cmp_k_w1, cmp_k_w2, cmp_v_pos, cmp_v_w1, cmp_v_w2, w_out,
           xa_pre_g, xa_post_g, mem_g, xa_wq, xa_wk, xa_wv, xa_wo, mlp_pre_g, mlp_post_g, mlp_w1, mlp_w2):
    B, S, _ = x.shape
    assert S % ROW_TILE == 0 and S // SEL_BLOCK >= N_SELECT and S // SEL_BLOCK <= LANES
    T = B * S
    ncp = S // CMP_STRIDE
    row2 = lambda a: a.reshape(1, -1)

    tw, tc = _bias_tables(rel_bias)
    twt = tw.reshape(N_KV, QROWS, WINDOW + QBLK).transpose(0, 2, 1)
    tct = tc.reshape(N_KV, QROWS, LANES).transpose(0, 2, 1)
    eblk, ovlt, place = _nsa_tables(S)

    pad_keys = (np.arange(S + KTILE)[:, None] >= S) & (np.arange(LANES)[None, :] == HEAD_DIM)
    keys0 = jnp.broadcast_to(jnp.asarray(pad_keys, BF16), (B, N_KV, S + KTILE, LANES))
    vals0 = jnp.zeros((B, N_KV, VROWS, S + KTILE), BF16)

    for l in range(DEPTH):
        w_a, w_bt = _in_proj_weights(w_in[l])
        up, uc, kcr, ks, kw, qt, vst, vwt, gates = _in_proj(x, row2(mix_pre_g[l]), w_a, w_bt,
                                                            keys0, keys0, vals0, vals0)

        pw_bd = jnp.zeros((POOL_WIDTH, POOL_WIDTH), F32)
        for g in range(len(POOL_WINDOWS)):
            pw_bd = lax.dynamic_update_slice(pw_bd, pool_w[l, g], (g * POOL_GROUP, g * POOL_GROUP))
        yp, yc = _mixers(up, uc, pw_bd.astype(BF16), row2(pool_scale[l]), conv_w[l], row2(conv_b[l]),
                         row2(conv_ln_g[l]), row2(conv_ln_b[l]), conv_pw[l].astype(BF16))

        cpos = jnp.stack([cmp_k_pos[l], cmp_v_pos[l]]).reshape(2, 2, 1, CMP_STRIDE * HEAD_DIM)
        cw1 = jnp.stack([cmp_k_w1[l], cmp_v_w1[l]]).astype(BF16)
        kc, vct = _compress(kcr, cpos, cw1, cmp_k_w2[l].astype(BF16), cmp_v_w2[l].T.astype(BF16))

        ynt = _nsa(qt, kc, vct, eblk, ks, vst, kw, vwt, gates, twt, tct, ovlt, place)

        km, vm = _mem_kv(mem, row2(mem_g[l]), xa_wk[l].astype(BF16), xa_wv[l].astype(BF16))
        x = _mix_xattn(x, yp, yc, ynt, w_out[l].astype(BF16), row2(mix_post_g[l]), row2(xa_pre_g[l]),
                       xa_wq[l].astype(BF16), km, vm, xa_wo[l].astype(BF16), row2(xa_post_g[l]))

        x = _mlp(x.reshape(T, D_MODEL), row2(mlp_pre_g[l]), mlp_w1[l].astype(BF16), mlp_w2[l].astype(BF16),
                 row2(mlp_post_g[l])).reshape(B, S, D_MODEL)
    return x
```

```python
import functools
import math

import numpy as np
import jax
import jax.numpy as jnp
from jax import lax
from jax.experimental import pallas as pl
from jax.experimental.pallas import tpu as pltpu

F32 = jnp.float32
BF16 = jnp.bfloat16

D_MODEL = 1024
DEPTH = 2
POOL_WIDTH = 256
POOL_WINDOWS = (2, 4, 8, 16)
POOL_GROUP = 64
CONV_WIDTH = 256
CONV_KSIZE = 31
NSA_WIDTH = 512
HEAD_DIM = 64
N_HEADS = 8
N_KV = 2
Q_PER_KV = 4
CMP_LEN = 32
CMP_STRIDE = 16
CMP_HIDDEN = 256
SEL_BLOCK = 64
N_SELECT = 16
WINDOW = 512
N_BRANCH = 3
N_BUCKETS = 32
MAX_EXACT = 16
MAX_DISTANCE = 128
XA_HEADS = 4
XA_HEAD_DIM = 256
D_FF = 4096
EPS = 1e-6
NEG_INF = -1e30
FORCE = 1e30

LANES = 128
ROW_TILE = 512
XA_SUB = 512
XA_TILE = 1024
QBLK = 256
KTILE = 256
QROWS = Q_PER_KV * QBLK
HALO = 32
CNEAR = 24
CNEAR_LO = 8
NEG_BF16 = -(2.0 ** 100)
M_INIT = -1e29
TAKEN = -3e38
N_FORCED = 3
LOG2E = 1.4426950408889634
FAR_UNROLLS = (8, 4, 2, 1)
VROWS = 80
VMEM_LIMIT = 56 * 1024 * 1024


def _rms(x, g):
    return x * lax.rsqrt(jnp.mean(x * x, axis=-1, keepdims=True) + EPS) * g


def _dot(a, b):
    return jnp.dot(a, b, preferred_element_type=F32)


def _dot_nt(a, b):
    return lax.dot_general(a, b, (((1,), (1,)), ((), ())), preferred_element_type=F32)


def _params(*sem):
    return pltpu.CompilerParams(dimension_semantics=sem, vmem_limit_bytes=VMEM_LIMIT)


def _in_proj_weights(w):
    z64 = jnp.zeros((D_MODEL, LANES - HEAD_DIM), w.dtype)
    z4 = jnp.zeros((D_MODEL, 4), w.dtype)
    kv0 = POOL_WIDTH + 2 * CONV_WIDTH + NSA_WIDTH
    col = lambda j, g: w[:, kv0 + j * 128 + g * HEAD_DIM:kv0 + j * 128 + (g + 1) * HEAD_DIM]
    w_a = jnp.concatenate([w[:, 0:768], col(2, 0), z64, col(2, 1), z64, col(4, 0), z64, col(4, 1), z64,
                           w[:, kv0:kv0 + 256]], axis=1)
    gw = w[:, kv0 + 768:]
    w_b = jnp.concatenate([w[:, 768:1280], w[:, kv0 + 384:kv0 + 512], w[:, kv0 + 640:kv0 + 768],
                           gw[:, 0:12], z4, gw[:, 12:24], z4], axis=1)
    return w_a.astype(BF16), w_b.T.astype(BF16)


def _in_proj_kernel(x_ref, g_ref, wa_ref, wbt_ref, ks_in, kw_in, vs_in, vw_in,
                    pool_ref, conv_ref, kc_ref, ks_ref, kw_ref, qt_ref, vs_ref, vw_ref, gate_ref):
    del ks_in, kw_in, vs_in, vw_in
    tm = x_ref.shape[1]
    h = _rms(x_ref[0], g_ref[...]).astype(BF16)
    pool_ref[0] = _dot(h, wa_ref[:, 0:256])
    conv_ref[0] = _dot(h, wa_ref[:, 256:768])
    ksw = _dot(h, wa_ref[:, 768:1280]).astype(BF16)
    kvc = _dot(h, wa_ref[:, 1280:1536])
    for g in range(N_KV):
        ks_ref[0, g] = ksw[:, g * LANES:(g + 1) * LANES]
        kw_ref[0, g] = ksw[:, (N_KV + g) * LANES:(N_KV + g + 1) * LANES]
        for kv in range(2):
            kc_ref[0, kv, g] = kvc[:, (kv * N_KV + g) * HEAD_DIM:(kv * N_KV + g + 1) * HEAD_DIM]

    pt = _dot_nt(wbt_ref[...], h)
    qs = (pt[0:NSA_WIDTH] * (HEAD_DIM ** -0.5 * LOG2E)).astype(BF16)
    ones_row = jnp.where(lax.broadcasted_iota(jnp.int32, (VROWS - HEAD_DIM, tm), 0) == 0, 1.0, 0.0).astype(BF16)
    gs = jax.nn.sigmoid(pt[NSA_WIDTH + 256:NSA_WIDTH + 288])
    for g in range(N_KV):
        for qb in range(tm // QBLK):
            cols = slice(qb * QBLK, (qb + 1) * QBLK)
            qt_ref[0, g, qb] = jnp.concatenate(
                [qs[(g * Q_PER_KV + r) * HEAD_DIM:(g * Q_PER_KV + r + 1) * HEAD_DIM, cols] for r in range(Q_PER_KV)],
                axis=1)
            gate_ref[0, qb, g] = gs[g * 16:g * 16 + Q_PER_KV * N_BRANCH, cols]
        v0 = NSA_WIDTH + g * HEAD_DIM
        vs_ref[0, g] = jnp.concatenate([pt[v0:v0 + HEAD_DIM].astype(BF16), ones_row], axis=0)
        vw_ref[0, g] = jnp.concatenate([pt[v0 + 128:v0 + 128 + HEAD_DIM].astype(BF16), ones_row], axis=0)


def _in_proj(x, g, w_a, w_bt, ks0, kw0, vs0, vw0):
    B, S, _ = x.shape
    tm = ROW_TILE
    nq = tm // QBLK
    full = lambda a: pl.BlockSpec(a.shape, lambda b, s: (0,) * a.ndim)
    anyspec = pl.BlockSpec(memory_space=pl.ANY)
    seq = lambda n: pl.BlockSpec((1, tm, n), lambda b, s: (b, s, 0))
    keys = pl.BlockSpec((1, N_KV, tm, LANES), lambda b, s: (b, 0, s, 0))
    vals = pl.BlockSpec((1, N_KV, VROWS, tm), lambda b, s: (b, 0, 0, s))
    return pl.pallas_call(
        _in_proj_kernel,
        grid=(B, S // tm),
        in_specs=[seq(D_MODEL), full(g), full(w_a), full(w_bt), anyspec, anyspec, anyspec, anyspec],
        out_specs=[
            seq(POOL_WIDTH), seq(2 * CONV_WIDTH),
            pl.BlockSpec((1, 2, N_KV, tm, HEAD_DIM), lambda b, s: (b, 0, 0, s, 0)),
            keys, keys,
            pl.BlockSpec((1, N_KV, nq, HEAD_DIM, QROWS), lambda b, s: (b, 0, s, 0, 0)),
            vals, vals,
            pl.BlockSpec((1, nq, N_KV, Q_PER_KV * N_BRANCH, QBLK), lambda b, s: (b, s, 0, 0, 0)),
        ],
        out_shape=[
            jax.ShapeDtypeStruct((B, S, POOL_WIDTH), F32),
            jax.ShapeDtypeStruct((B, S, 2 * CONV_WIDTH), F32),
            jax.ShapeDtypeStruct((B, 2, N_KV, S, HEAD_DIM), F32),
            jax.ShapeDtypeStruct(ks0.shape, BF16),
            jax.ShapeDtypeStruct(kw0.shape, BF16),
            jax.ShapeDtypeStruct((B, N_KV, S // QBLK, HEAD_DIM, QROWS), BF16),
            jax.ShapeDtypeStruct(vs0.shape, BF16),
            jax.ShapeDtypeStruct(vw0.shape, BF16),
            jax.ShapeDtypeStruct((B, S // QBLK, N_KV, Q_PER_KV * N_BRANCH, QBLK), F32),
        ],
        input_output_aliases={4: 3, 5: 4, 6: 6, 7: 7},
        compiler_params=_params("parallel", "parallel"),
        name="in_proj",
    )(x, g, w_a, w_bt, ks0, kw0, vs0, vw0)


def _mixers_kernel(up_ref, uc_ref, pw_ref, ps_ref, cw_ref, cb_ref, lg_ref, lb_ref, cpw_ref,
                   yp_ref, yc_ref, ubuf, hbuf):
    s = pl.program_id(1)
    ts = up_ref.shape[1]

    @pl.when(s == 0)
    def _():
        ubuf[0:HALO, :] = jnp.zeros((HALO, POOL_WIDTH), F32)
        hbuf[0:HALO, :] = jnp.zeros((HALO, CONV_WIDTH), F32)

    ext = HALO + ts
    back = lambda a, k: a + pltpu.roll(a, k, 0)

    u = up_ref[0]
    ubuf[HALO:ext, :] = u
    xl = ubuf[:, 0:128]
    xh = ubuf[:, 128:256]
    s2l = back(xl, 1)
    s4l = back(s2l, 2)
    s8h = back(back(back(xh, 1), 2), 4)
    s16h = back(s8h, 8)
    t1 = (s * ts + lax.broadcasted_iota(jnp.int32, (ts, 128), 0) + 1).astype(F32)
    lane = lax.broadcasted_iota(jnp.int32, (ts, 128), 1)
    first = lane < POOL_GROUP
    sum_lo = jnp.where(first, s2l[HALO:ext], s4l[HALO:ext])
    den_lo = jnp.minimum(t1, jnp.where(first, 2.0, 4.0))
    sum_hi = jnp.where(first, s8h[HALO:ext], s16h[HALO:ext])
    den_hi = jnp.minimum(t1, jnp.where(first, 8.0, 16.0))
    d = jnp.concatenate([sum_lo / den_lo - u[:, 0:128], sum_hi / den_hi - u[:, 128:256]], axis=1)
    yp_ref[0] = (_dot(d.astype(BF16), pw_ref[...]) * ps_ref[...]).astype(BF16)
    ubuf[0:HALO, :] = ubuf[ts:ext, :]

    uc = uc_ref[0]
    hbuf[HALO:ext, :] = uc[:, 0:CONV_WIDTH] * jax.nn.sigmoid(uc[:, CONV_WIDTH:])
    halves = []
    for c0 in range(0, CONV_WIDTH, LANES):
        hx = hbuf[:, c0:c0 + LANES]
        ahead = [hx] + [pltpu.roll(hx, ext - r, 0) for r in range(1, 8)]
        acc_h = jnp.zeros((ts, LANES), F32) + cb_ref[:, c0:c0 + LANES]
        for j in range(CONV_KSIZE):
            a8, r = divmod(HALO - (CONV_KSIZE - 1) + j, 8)
            acc_h = acc_h + ahead[r][8 * a8:8 * a8 + ts] * cw_ref[j:j + 1, c0:c0 + LANES]
        halves.append(acc_h)
    acc = jnp.concatenate(halves, axis=1)
    mu = jnp.mean(acc, axis=-1, keepdims=True)
    var = jnp.mean(jnp.square(acc - mu), axis=-1, keepdims=True)
    y = (acc - mu) * lax.rsqrt(var + EPS) * lg_ref[...] + lb_ref[...]
    y = y * jax.nn.sigmoid(y)
    yc_ref[0] = _dot(y.astype(BF16), cpw_ref[...]).astype(BF16)
    hbuf[0:HALO, :] = hbuf[ts:ts + HALO, :]


def _mixers(up, uc, pw, ps, cw, cb, lg, lb, cpw):
    B, S, _ = up.shape
    ts = ROW_TILE
    seq = lambda n: pl.BlockSpec((1, ts, n), lambda b, s: (b, s, 0))
    full = lambda a: pl.BlockSpec(a.shape, lambda b, s: (0,) * a.ndim)
    return pl.pallas_call(
        _mixers_kernel,
        grid=(B, S // ts),
        in_specs=[seq(256), seq(512)] + [full(a) for a in (pw, ps, cw, cb, lg, lb, cpw)],
        out_specs=[seq(256), seq(256)],
        out_shape=[jax.ShapeDtypeStruct((B, S, 256), BF16)] * 2,
        scratch_shapes=[pltpu.VMEM((HALO + ts, POOL_WIDTH), F32), pltpu.VMEM((HALO + ts, CONV_WIDTH), F32)],
        compiler_params=_params("parallel", "arbitrary"),
        name="mixers",
    )(up, uc, pw, ps, cw, cb, lg, lb, cpw)


def _compress_kernel(r_ref, pos_ref, w1_ref, w2k_ref, w2vt_ref, kc_ref, vct_ref):
    ncp = r_ref.shape[3] // CMP_STRIDE
    half = CMP_STRIDE * HEAD_DIM

    def hidden(kv):
        r = jnp.concatenate([r_ref[0, kv, 0, pl.ds(l, ncp, stride=CMP_STRIDE), :] for l in range(CMP_STRIDE)],
                            axis=1)
        a = _dot((r + pos_ref[kv, 0]).astype(BF16), w1_ref[kv, 0:half, :])
        b = _dot((r + pos_ref[kv, 1]).astype(BF16), w1_ref[kv, half:2 * half, :])
        return jax.nn.gelu(a + pltpu.roll(b, ncp - 1, 0)).astype(BF16)

    kc = _dot(hidden(0), w2k_ref[...]).astype(BF16)
    kc_ref[0, 0] = jnp.concatenate([kc, jnp.zeros((ncp, LANES - HEAD_DIM), BF16)], axis=1)
    ones_row = jnp.where(lax.broadcasted_iota(jnp.int32, (VROWS - HEAD_DIM, ncp), 0) == 0, 1.0, 0.0).astype(BF16)
    vct_ref[0, 0] = jnp.concatenate([_dot_nt(w2vt_ref[...], hidden(1)).astype(BF16), ones_row], axis=0)


def _compress(r, pos, w1, w2k, w2vt):
    B, _, G, S, _ = r.shape
    ncp = S // CMP_STRIDE
    full = lambda a: pl.BlockSpec(a.shape, lambda b, g: (0,) * a.ndim)
    return pl.pallas_call(
        _compress_kernel,
        grid=(B, G),
        in_specs=[pl.BlockSpec((1, 2, 1, S, HEAD_DIM), lambda b, g: (b, 0, g, 0, 0)),
                  full(pos), full(w1), full(w2k), full(w2vt)],
        out_specs=[pl.BlockSpec((1, 1, ncp, LANES), lambda b, g: (b, g, 0, 0)),
                   pl.BlockSpec((1, 1, VROWS, ncp), lambda b, g: (b, g, 0, 0))],
        out_shape=[jax.ShapeDtypeStruct((B, G, ncp, LANES), BF16),
                   jax.ShapeDtypeStruct((B, G, VROWS, ncp), BF16)],
        compiler_params=_params("parallel", "parallel"),
        name="compress",
    )(r, pos, w1, w2k, w2vt)


def _rel_bucket_np(dist):
    n = np.maximum(dist, 0)
    nf = np.maximum(n, 1).astype(np.float64)
    large = MAX_EXACT + (np.log(nf / MAX_EXACT) / math.log(MAX_DISTANCE / MAX_EXACT)
                         * (N_BUCKETS - MAX_EXACT)).astype(np.int64)
    return np.where(n < MAX_EXACT, n, np.minimum(large, N_BUCKETS - 1)).astype(np.int32)


def _static_maps():
    qq = np.arange(QBLK)[:, None]
    kk = np.arange(WINDOW + QBLK)[None, :]
    dist = qq + WINDOW - kk
    wmap = np.where((dist >= 0) & (dist < WINDOW), _rel_bucket_np(dist), -1)
    u = np.arange(LANES)[None, :]
    dist_c = qq - CMP_STRIDE * (u - CNEAR_LO) - (CMP_LEN - 1)
    cmap = np.where((dist_c >= 0) & (u < CNEAR), _rel_bucket_np(dist_c), -1)
    cmap = np.where(u < CNEAR, cmap, -2)
    return wmap.astype(np.int32), cmap.astype(np.int32)


def _tables_kernel(bias_ref, wmap_ref, cmap_ref, tw_ref, tc_ref):
    h = pl.program_id(0)
    far = bias_ref[N_BUCKETS - 1, h]

    def lookup(m):
        out = jnp.where(m == -1, NEG_INF, 0.0).astype(F32)
        for b in range(N_BUCKETS):
            out = jnp.where(m == b, (bias_ref[b, h] - far) * LOG2E, out)
        return out

    tw_ref[0] = lookup(wmap_ref[...])
    cmap = cmap_ref[...]
    val = lookup(cmap)
    hi = val.astype(BF16)
    lo = (val - hi.astype(F32)).astype(BF16)
    col = lax.broadcasted_iota(jnp.int32, cmap.shape, 1)
    hi_part = jnp.where(col < CNEAR, hi.astype(F32), 0.0)
    lo_part = pltpu.roll(jnp.where(col < CNEAR, lo.astype(F32), 0.0), CNEAR, 1)
    one = jnp.where(col == 2 * CNEAR, 1.0, 0.0)
    tc_ref[0] = (hi_part + lo_part + one).astype(BF16)


def _bias_tables(rel_bias):
    wmap, cmap = _static_maps()
    nw = WINDOW + QBLK
    return pl.pallas_call(
        _tables_kernel,
        grid=(N_HEADS,),
        in_specs=[
            pl.BlockSpec(memory_space=pltpu.SMEM),
            pl.BlockSpec((QBLK, nw), lambda h: (0, 0)),
            pl.BlockSpec((QBLK, LANES), lambda h: (0, 0)),
        ],
        out_specs=[
            pl.BlockSpec((1, QBLK, nw), lambda h: (h, 0, 0)),
            pl.BlockSpec((1, QBLK, LANES), lambda h: (h, 0, 0)),
        ],
        out_shape=[
            jax.ShapeDtypeStruct((N_HEADS, QBLK, nw), F32),
            jax.ShapeDtypeStruct((N_HEADS, QBLK, LANES), BF16),
        ],
        compiler_params=_params("parallel"),
        name="bias_tables",
    )(rel_bias, jnp.asarray(wmap), jnp.asarray(cmap))


def _nsa_tables(S):
    ncp = S // CMP_STRIDE
    nb = LANES
    pos = np.arange(S + KTILE)
    eblk = (pos[:, None] // SEL_BLOCK == np.arange(nb)[None, :]) & (pos[:, None] < S)
    cs = np.arange(ncp)[None, :] * CMP_STRIDE
    ss = np.arange(nb)[:, None] * SEL_BLOCK
    ovlt = (cs < ss + SEL_BLOCK) & (cs + CMP_LEN > ss) & (np.arange(ncp)[None, :] < ncp - 1) & (ss < S)
    rel = np.arange(2 * ncp)[:, None] - ncp + CNEAR_LO
    u = np.arange(LANES)[None, :]
    place = np.where((rel == u) & (u < CNEAR), 1.0, 0.0)
    place = np.where((rel == u - CNEAR) & (u >= CNEAR) & (u < 2 * CNEAR), 1.0, place)
    place = np.where((u == 2 * CNEAR) & (rel >= CNEAR), NEG_BF16, place)
    return jnp.asarray(eblk, BF16), jnp.asarray(ovlt, BF16), jnp.asarray(place, BF16)


def _nsa_kernel(qt_ref, kc_ref, vc_ref, eb_ref, ks_ref, vs_ref, kw_ref, vw_ref, gate_ref, tw_ref, tc_ref,
                ovl_ref, place_ref, o_ref, qaug_ref, s0_ref, s1_ref, m_ref, acc_ref, oc_next_ref, msel_next_ref):
    i = pl.program_id(2)
    ncp = kc_ref.shape[2]
    nb = ovl_ref.shape[0]
    pad_tile = ks_ref.shape[2] // KTILE - 1
    col_max = lambda a: jnp.max(a, axis=0, keepdims=True)
    prob = lambda a: jnp.exp2(a).astype(BF16)
    tile_off = lambda t, ok: pl.multiple_of(jnp.where(ok, t, pad_tile) * KTILE, KTILE)

    def query_operand(q_blk):
        row = lax.broadcasted_iota(jnp.int32, (LANES - HEAD_DIM, QROWS), 0)
        return jnp.concatenate([q_blk, jnp.where(row == 0, NEG_BF16, 0.0).astype(BF16)], axis=0)

    def select_operands(q_blk, blk):
        place = place_ref[pl.ds(pl.multiple_of(ncp - CMP_STRIDE * blk, CMP_STRIDE), ncp), :]
        return place, jnp.concatenate([query_operand(q_blk), tc_ref[0]], axis=0)

    def cmp_logits(place, rhs, c0, rows):
        return _dot(jnp.concatenate([kc_ref[0, 0, c0:c0 + rows, :], place[c0:c0 + rows]], axis=1), rhs)

    def cmp_values(c0, rows):
        return jnp.concatenate([vc_ref[0, 0, :, c0:c0 + rows], ovl_ref[:, c0:c0 + rows]], axis=0)

    def stage(state, s, vt):
        if state is None:
            m = jnp.maximum(col_max(s), M_INIT)
            return m, _dot(vt, prob(s - m))
        m_prev, acc = state
        m = jnp.maximum(m_prev, col_max(s))
        return m, jnp.exp2(m_prev - m) * acc + _dot(vt, prob(s - m))

    def select_store(acc, blk):
        inv_l = 1.0 / jnp.maximum(acc[HEAD_DIM:HEAD_DIM + 1, :], 1e-30)
        imp4 = acc[VROWS:VROWS + nb, :] * inv_l
        imp = imp4[:, 0:QBLK] + imp4[:, QBLK:2 * QBLK] + imp4[:, 2 * QBLK:3 * QBLK] + imp4[:, 3 * QBLK:4 * QBLK]
        jb = lax.broadcasted_iota(jnp.int32, (nb, QBLK), 0)
        qq = lax.broadcasted_iota(jnp.int32, (nb, QBLK), 1)
        cur = (QBLK // SEL_BLOCK) * blk + (qq >> 6)
        forced = (jb == 0) | (jb == cur) | (jb == cur - 1)
        sc = jnp.where(forced, TAKEN, jnp.where(jb <= cur, imp, NEG_INF))
        jio = jb.astype(F32)
        for _ in range(N_SELECT - N_FORCED):
            idx = jnp.min(jnp.where(sc == col_max(sc), jio, float(nb)), axis=0, keepdims=True)
            sc = jnp.where(jio == idx, TAKEN, sc)
        msel = jnp.where(sc == TAKEN, 0.0, NEG_BF16).astype(BF16)
        oc_next_ref[...] = acc[0:HEAD_DIM, :] * inv_l
        msel_next_ref[...] = jnp.concatenate([msel] * Q_PER_KV, axis=1)

    @pl.when(i == 0)
    def _():
        place, rhs = select_operands(qt_ref[0, 0, 0], 0)
        select_store(stage(None, cmp_logits(place, rhs, 0, ncp), cmp_values(0, ncp))[1], 0)

    qn = query_operand(qt_ref[0, 0, i])
    qaug_ref[nb:nb + LANES, :] = qn
    qaug_ref[0:nb, :] = msel_next_ref[...]
    o_c = oc_next_ref[...]

    def sel_logits(o):
        keys = jnp.concatenate([eb_ref[pl.ds(o, KTILE), :], ks_ref[0, 0, pl.ds(o, KTILE), :]], axis=1)
        return _dot(keys, qaug_ref[...])

    n_far = jnp.maximum(i - 1, 0)
    far_off = lambda t: tile_off(t, t < n_far)

    nxt = jnp.minimum(i + 1, pl.num_programs(2) - 1)
    place_n, rhs_n = select_operands(qt_ref[0, 0, nxt], nxt)
    ct = min(KTILE, ncp)
    jobs = [("cmp", ct, functools.partial(cmp_logits, place_n, rhs_n, c0, ct), functools.partial(cmp_values, c0, ct))
            for c0 in range(0, ncp, ct)]
    for tt in range(WINDOW // KTILE + 1):
        t = i - WINDOW // KTILE + tt
        o = tile_off(t, t >= 0)
        jobs.append(("win", KTILE,
                     lambda o=o, tt=tt: (_dot(kw_ref[0, 0, pl.ds(o, KTILE), :], qn)
                                         + tw_ref[0, tt * KTILE:(tt + 1) * KTILE, :]),
                     lambda o=o: vw_ref[0, 0, :, pl.ds(o, KTILE)]))
    for o, t0 in ((tile_off(i - 1, i >= 1), WINDOW - KTILE), (pl.multiple_of(i * KTILE, KTILE), WINDOW)):
        jobs.append(("sel", KTILE, lambda o=o, t0=t0: sel_logits(o) + tw_ref[0, t0:t0 + KTILE, :],
                     lambda o=o: vs_ref[0, 0, :, pl.ds(o, KTILE)]))
    jobs.append(("far", KTILE, lambda: sel_logits(far_off(0)), None))

    bufs = (s0_ref, s1_ref)
    bufs[0][0:jobs[0][1], :] = jobs[0][2]()
    state = {}
    for k, (branch, rows, _, values) in enumerate(jobs[:-1]):
        bufs[(k + 1) % 2][0:jobs[k + 1][1], :] = jobs[k + 1][2]()
        state[branch] = stage(state.get(branch), bufs[k % 2][0:rows, :], values())
    far_bufs = (bufs[(len(jobs) - 1) % 2], bufs[len(jobs) % 2])
    acc_w = state["win"][1]
    m_ref[...] = jnp.broadcast_to(state["sel"][0], m_ref.shape)
    acc_ref[...] = state["sel"][1]
    select_store(state["cmp"][1], nxt)

    def update(s_ref, t):
        o = far_off(t)
        sl = s_ref[...]
        m_prev = m_ref[...]
        m_next = jnp.maximum(m_prev, col_max(sl))
        alpha = jnp.exp2(m_prev - m_next)
        pt = prob(sl - m_next[0:1, :])
        acc_ref[...] = alpha[0:1, :] * acc_ref[...] + _dot(vs_ref[0, 0, :, pl.ds(o, KTILE)], pt)
        m_ref[...] = m_next

    def far_trips(first_tile, n_trips, unroll):
        def body(k, carry):
            for h in range(unroll):
                t = first_tile + unroll * k + h
                far_bufs[(h + 1) % 2][...] = sel_logits(far_off(t + 1))
                update(far_bufs[h % 2], t)
            return carry

        lax.fori_loop(0, n_trips, body, 0)

    done = 0
    for unroll in FAR_UNROLLS[:-1]:
        n_trips = (n_far - done) // unroll
        far_trips(done, n_trips, unroll)
        done = done + n_trips * unroll
    last = FAR_UNROLLS[-1]
    far_trips(done, (n_far - done + last - 1) // last, last)
    acc_s = acc_ref[...]

    gb = gate_ref[0, i, 0]
    gate = lambda br: jnp.concatenate(
        [gb[r * N_BRANCH + br:r * N_BRANCH + br + 1, :] for r in range(Q_PER_KV)], axis=1)
    out = (gate(0) * o_c
           + (gate(1) / acc_s[HEAD_DIM:HEAD_DIM + 1, :]) * acc_s[0:HEAD_DIM, :]
           + (gate(2) / acc_w[HEAD_DIM:HEAD_DIM + 1, :]) * acc_w[0:HEAD_DIM, :]).astype(BF16)
    q_off = pl.multiple_of(i * QBLK, QBLK)
    for r in range(Q_PER_KV):
        o_ref[0, r * HEAD_DIM:(r + 1) * HEAD_DIM, pl.ds(q_off, QBLK)] = out[:, r * QBLK:(r + 1) * QBLK]


def _nsa(qt, kc, vct, eblk, ks, vst, kw, vwt, gates, twt, tct, ovlt, place):
    B, G, NQ, _, _ = qt.shape
    S = NQ * QBLK
    nb = ovlt.shape[0]
    per_bg = lambda a: pl.BlockSpec((1, 1) + a.shape[2:], lambda b, g, i: (b, g) + (0,) * (a.ndim - 2))
    per_g = lambda a: pl.BlockSpec((1,) + a.shape[1:], lambda b, g, i: (g,) + (0,) * (a.ndim - 1))
    const = lambda a: pl.BlockSpec(a.shape, lambda b, g, i: (0,) * a.ndim)
    return pl.pallas_call(
        _nsa_kernel,
        grid=(B, G, NQ),
        in_specs=[per_bg(qt),
                  per_bg(kc), per_bg(vct), const(eblk), per_bg(ks), per_bg(vst), per_bg(kw), per_bg(vwt),
                  pl.BlockSpec((1, NQ, 1, Q_PER_KV * N_BRANCH, QBLK), lambda b, g, i: (b, 0, g, 0, 0)),
                  per_g(twt), per_g(tct), const(ovlt), const(place)],
        out_specs=pl.BlockSpec((1, Q_PER_KV * HEAD_DIM, S), lambda b, g, i: (b, g, 0)),
        out_shape=jax.ShapeDtypeStruct((B, NSA_WIDTH, S), BF16),
        scratch_shapes=[pltpu.VMEM((nb + LANES, QROWS), BF16),
                        pltpu.VMEM((KTILE, QROWS), F32), pltpu.VMEM((KTILE, QROWS), F32),
                        pltpu.VMEM((8, QROWS), F32), pltpu.VMEM((VROWS, QROWS), F32),
                        pltpu.VMEM((HEAD_DIM, QROWS), F32), pltpu.VMEM((nb, QROWS), BF16)],
        compiler_params=_params("parallel", "parallel", "arbitrary"),
        name="nsa",
    )(qt, kc, vct, eblk, ks, vst, kw, vwt, gates, twt, tct, ovlt, place)


def _mem_kv_kernel(mem_ref, g_ref, wk_ref, wv_ref, k_ref, v_ref):
    m = _rms(mem_ref[0], g_ref[...]).astype(BF16)
    k_ref[0] = _dot(m, wk_ref[...]).astype(BF16)
    v_ref[0] = _dot(m, wv_ref[...]).astype(BF16)


def _mem_kv(mem, g, wk, wv):
    B, M, _ = mem.shape
    full = lambda a: pl.BlockSpec(a.shape, lambda b: (0,) * a.ndim)
    per_b = pl.BlockSpec((1, M, D_MODEL), lambda b: (b, 0, 0))
    return pl.pallas_call(
        _mem_kv_kernel,
        grid=(B,),
        in_specs=[per_b, full(g), full(wk), full(wv)],
        out_specs=[per_b, per_b],
        out_shape=[jax.ShapeDtypeStruct((B, M, D_MODEL), BF16)] * 2,
        compiler_params=_params("parallel"),
        name="mem_kv",
    )(mem, g, wk, wv)


def _mix_xattn_kernel(x_ref, yp_ref, yc_ref, ynt_ref, wmix_ref, gmix_ref,
                      gpre_ref, wq_ref, k_ref, v_ref, wo_ref, gpost_ref, o_ref):
    ts = x_ref.shape[1]
    subs = [slice(r, r + XA_SUB) for r in range(0, ts, XA_SUB)]
    mix = [_dot(yp_ref[0, r, :], wmix_ref[0:256, :]) + _dot(yc_ref[0, r, :], wmix_ref[256:512, :])
           + lax.dot_general(ynt_ref[0, :, r], wmix_ref[512:1024, :], (((0,), (0,)), ((), ())),
                             preferred_element_type=F32) for r in subs]
    x = [x_ref[0, r, :] + _rms(m, gmix_ref[...]) for r, m in zip(subs, mix)]
    h = [_rms(xr, gpre_ref[...]).astype(BF16) for xr in x]
    qx = [(_dot(hr, wq_ref[...]) * (XA_HEAD_DIM ** -0.5)).astype(BF16) for hr in h]
    attn = []
    for qr in qx:
        outs = []
        for hh in range(XA_HEADS):
            sl = slice(hh * XA_HEAD_DIM, (hh + 1) * XA_HEAD_DIM)
            s = _dot_nt(qr[:, sl], k_ref[0, :, sl])
            p = jnp.exp(s - jnp.max(s, axis=1, keepdims=True))
            p = p * (1.0 / jnp.sum(p, axis=1, keepdims=True))
            outs.append(_dot(p.astype(BF16), v_ref[0, :, sl]).astype(BF16))
        attn.append(jnp.concatenate(outs, axis=1))
    for r, xr, o in zip(subs, x, attn):
        o_ref[0, r, :] = xr + _rms(_dot(o, wo_ref[...]), gpost_ref[...])


def _mix_xattn(x, yp, yc, ynt, wmix, gmix, gpre, wq, km, vm, wo, gpost):
    B, S, _ = x.shape
    M = km.shape[1]
    ts = XA_TILE
    full = lambda a: pl.BlockSpec(a.shape, lambda b, s: (0,) * a.ndim)
    seq = lambda n: pl.BlockSpec((1, ts, n), lambda b, s: (b, s, 0))
    per_b = pl.BlockSpec((1, M, D_MODEL), lambda b, s: (b, 0, 0))
    return pl.pallas_call(
        _mix_xattn_kernel,
        grid=(B, S // ts),
        in_specs=[seq(D_MODEL), seq(POOL_WIDTH), seq(CONV_WIDTH),
                  pl.BlockSpec((1, NSA_WIDTH, ts), lambda b, s: (b, 0, s)), full(wmix), full(gmix),
                  full(gpre), full(wq), per_b, per_b, full(wo), full(gpost)],
        out_specs=seq(D_MODEL),
        out_shape=jax.ShapeDtypeStruct((B, S, D_MODEL), F32),
        compiler_params=_params("parallel", "parallel"),
        name="mix_xattn",
    )(x, yp, yc, ynt, wmix, gmix, gpre, wq, km, vm, wo, gpost)


def _mlp_kernel(x_ref, gpre_ref, w1_ref, w2_ref, gpost_ref, o_ref):
    x = x_ref[...]
    h = _rms(x, gpre_ref[...]).astype(BF16)
    y = jnp.zeros(x.shape, F32)
    chunk = 1024
    for c in range(D_FF // chunk):
        a = jnp.maximum(_dot(h, w1_ref[:, c * chunk:(c + 1) * chunk]), 0.0)
        y = y + _dot((a * a).astype(BF16), w2_ref[c * chunk:(c + 1) * chunk, :])
    o_ref[...] = x + _rms(y, gpost_ref[...])


def _mlp(xf, gpre, w1, w2, gpost):
    T = xf.shape[0]
    row = pl.BlockSpec((ROW_TILE, D_MODEL), lambda i: (i, 0))
    full = lambda a: pl.BlockSpec(a.shape, lambda i: (0,) * a.ndim)
    return pl.pallas_call(
        _mlp_kernel,
        grid=(T // ROW_TILE,),
        in_specs=[row, full(gpre), full(w1), full(w2), full(gpost)],
        out_specs=row,
        out_shape=jax.ShapeDtypeStruct((T, D_MODEL), F32),
        compiler_params=_params("parallel"),
        name="mlp",
    )(xf, gpre, w1, w2, gpost)


def kernel(x, mem, rel_bias, mix_pre_g, mix_post_g, w_in, pool_w, pool_scale, conv_w, conv_b, conv_ln_g,
           conv_ln_b, conv_pw, cmp_k_pos, cmp_k_w1, cmp_k_w2, cmp_v_pos, cmp_v_w1, cmp_v_w2, w_out,
           xa_pre_g, xa_post_g, mem_g, xa_wq, xa_wk, xa_wv, xa_wo, mlp_pre_g, mlp_post_g, mlp_w1, mlp_w2):
    B, S, _ = x.shape
    assert S % ROW_TILE == 0 and S // SEL_BLOCK >= N_SELECT and S // SEL_BLOCK <= LANES
    T = B * S
    ncp = S // CMP_STRIDE
    row2 = lambda a: a.reshape(1, -1)

    tw, tc = _bias_tables(rel_bias)
    twt = tw.reshape(N_KV, QROWS, WINDOW + QBLK).transpose(0, 2, 1)
    tct = tc.reshape(N_KV, QROWS, LANES).transpose(0, 2, 1)
    eblk, ovlt, place = _nsa_tables(S)

    pad_keys = (np.arange(S + KTILE)[:, None] >= S) & (np.arange(LANES)[None, :] == HEAD_DIM)
    keys0 = jnp.broadcast_to(jnp.asarray(pad_keys, BF16), (B, N_KV, S + KTILE, LANES))
    vals0 = jnp.zeros((B, N_KV, VROWS, S + KTILE), BF16)

    for l in range(DEPTH):
        w_a, w_bt = _in_proj_weights(w_in[l])
        up, uc, kcr, ks, kw, qt, vst, vwt, gates = _in_proj(x, row2(mix_pre_g[l]), w_a, w_bt,
                                                            keys0, keys0, vals0, vals0)

        pw_bd = jnp.zeros((POOL_WIDTH, POOL_WIDTH), F32)
        for g in range(len(POOL_WINDOWS)):
            pw_bd = lax.dynamic_update_slice(pw_bd, pool_w[l, g], (g * POOL_GROUP, g * POOL_GROUP))
        yp, yc = _mixers(up, uc, pw_bd.astype(BF16), row2(pool_scale[l]), conv_w[l], row2(conv_b[l]),
                         row2(conv_ln_g[l]), row2(conv_ln_b[l]), conv_pw[l].astype(BF16))

        cpos = jnp.stack([cmp_k_pos[l], cmp_v_pos[l]]).reshape(2, 2, 1, CMP_STRIDE * HEAD_DIM)
        cw1 = jnp.stack([cmp_k_w1[l], cmp_v_w1[l]]).astype(BF16)
        kc, vct = _compress(kcr, cpos, cw1, cmp_k_w2[l].astype(BF16), cmp_v_w2[l].T.astype(BF16))

        ynt = _nsa(qt, kc, vct, eblk, ks, vst, kw, vwt, gates, twt, tct, ovlt, place)

        km, vm = _mem_kv(mem, row2(mem_g[l]), xa_wk[l].astype(BF16), xa_wv[l].astype(BF16))
        x = _mix_xattn(x, yp, yc, ynt, w_out[l].astype(BF16), row2(mix_post_g[l]), row2(xa_pre_g[l]),
                       xa_wq[l].astype(BF16), km, vm, xa_wo[l].astype(BF16), row2(xa_post_g[l]))

        x = _mlp(x.reshape(T, D_MODEL), row2(mlp_pre_g[l]), mlp_w1[l].astype(BF16), mlp_w2[l].astype(BF16),
                 row2(mlp_post_g[l])).reshape(B, S, D_MODEL)
    return x
```

```python
import functools
import math

import numpy as np
import jax
import jax.numpy as jnp
from jax import lax
from jax.experimental import pallas as pl
from jax.experimental.pallas import tpu as pltpu

F32 = jnp.float32
BF16 = jnp.bfloat16

D_MODEL = 1024
DEPTH = 2
POOL_WIDTH = 256
POOL_WINDOWS = (2, 4, 8, 16)
POOL_GROUP = 64
CONV_WIDTH = 256
CONV_KSIZE = 31
NSA_WIDTH = 512
HEAD_DIM = 64
N_HEADS = 8
N_KV = 2
Q_PER_KV = 4
CMP_LEN = 32
CMP_STRIDE = 16
CMP_HIDDEN = 256
SEL_BLOCK = 64
N_SELECT = 16
WINDOW = 512
N_BRANCH = 3
N_BUCKETS = 32
MAX_EXACT = 16
MAX_DISTANCE = 128
XA_HEADS = 4
XA_HEAD_DIM = 256
D_FF = 4096
EPS = 1e-6
NEG_INF = -1e30
FORCE = 1e30

LANES = 128
ROW_TILE = 512
XA_SUB = 512
XA_TILE = 1024
QBLK = 256
KTILE = 256
QROWS = Q_PER_KV * QBLK
HALO = 32
CNEAR = 24
CNEAR_LO = 8
NEG_BF16 = -(2.0 ** 100)
M_INIT = -1e29
TAKEN = -3e38
N_FORCED = 3
LOG2E = 1.4426950408889634
FAR_UNROLLS = (8, 4, 2, 1)
VROWS = 80
VMEM_LIMIT = 56 * 1024 * 1024


def _rms(x, g):
    return x * lax.rsqrt(jnp.mean(x * x, axis=-1, keepdims=True) + EPS) * g


def _dot(a, b):
    return jnp.dot(a, b, preferred_element_type=F32)


def _dot_nt(a, b):
    return lax.dot_general(a, b, (((1,), (1,)), ((), ())), preferred_element_type=F32)


def _params(*sem):
    return pltpu.CompilerParams(dimension_semantics=sem, vmem_limit_bytes=VMEM_LIMIT)


def _in_proj_weights(w):
    z4 = jnp.zeros((D_MODEL, 4), w.dtype)
    kv0 = POOL_WIDTH + 2 * CONV_WIDTH + NSA_WIDTH
    w_a = jnp.concatenate([w[:, 0:768], w[:, kv0 + 256:kv0 + 384], w[:, kv0 + 512:kv0 + 640],
                           w[:, kv0:kv0 + 256]], axis=1)
    gw = w[:, kv0 + 768:]
    w_b = jnp.concatenate([w[:, 768:1280], w[:, kv0 + 384:kv0 + 512], w[:, kv0 + 640:kv0 + 768],
                           gw[:, 0:12], z4, gw[:, 12:24], z4], axis=1)
    return w_a.astype(BF16), w_b.T.astype(BF16)


def _in_proj_kernel(x_ref, g_ref, wa_ref, wbt_ref,
                    pool_ref, conv_ref, kc_ref, ks_ref, kw_ref, qt_ref, vs_ref, vw_ref, gate_ref):
    tm = x_ref.shape[1]
    h = _rms(x_ref[0], g_ref[...]).astype(BF16)
    pool_ref[0] = _dot(h, wa_ref[:, 0:256])
    conv_ref[0] = _dot(h, wa_ref[:, 256:768])
    ksw = _dot(h, wa_ref[:, 768:1024]).astype(BF16)
    kvc = _dot(h, wa_ref[:, 1024:1280])
    lane_pad = jnp.zeros((tm, LANES - HEAD_DIM), BF16)
    for g in range(N_KV):
        ks_ref[0, g] = jnp.concatenate([ksw[:, g * HEAD_DIM:(g + 1) * HEAD_DIM], lane_pad], axis=1)
        kw_ref[0, g] = jnp.concatenate([ksw[:, (N_KV + g) * HEAD_DIM:(N_KV + g + 1) * HEAD_DIM], lane_pad], axis=1)
        for kv in range(2):
            kc_ref[0, kv, g] = kvc[:, (kv * N_KV + g) * HEAD_DIM:(kv * N_KV + g + 1) * HEAD_DIM]

    pt = _dot_nt(wbt_ref[...], h)
    qs = (pt[0:NSA_WIDTH] * (HEAD_DIM ** -0.5 * LOG2E)).astype(BF16)
    ones_row = jnp.where(lax.broadcasted_iota(jnp.int32, (VROWS - HEAD_DIM, tm), 0) == 0, 1.0, 0.0).astype(BF16)
    gs = jax.nn.sigmoid(pt[NSA_WIDTH + 256:NSA_WIDTH + 288])
    for g in range(N_KV):
        for qb in range(tm // QBLK):
            cols = slice(qb * QBLK, (qb + 1) * QBLK)
            qt_ref[0, g, qb] = jnp.concatenate(
                [qs[(g * Q_PER_KV + r) * HEAD_DIM:(g * Q_PER_KV + r + 1) * HEAD_DIM, cols] for r in range(Q_PER_KV)],
                axis=1)
            gate_ref[0, qb, g] = gs[g * 16:g * 16 + Q_PER_KV * N_BRANCH, cols]
        v0 = NSA_WIDTH + g * HEAD_DIM
        vs_ref[0, g] = jnp.concatenate([pt[v0:v0 + HEAD_DIM].astype(BF16), ones_row], axis=0)
        vw_ref[0, g] = jnp.concatenate([pt[v0 + 128:v0 + 128 + HEAD_DIM].astype(BF16), ones_row], axis=0)


def _pad_tiles_kernel(ks_in, kw_in, vs_in, vw_in, ks_ref, kw_ref, vs_ref, vw_ref):
    del ks_in, kw_in, vs_in, vw_in
    lane = lax.broadcasted_iota(jnp.int32, ks_ref.shape, 3)
    pad_keys = jnp.where(lane == HEAD_DIM, 1.0, 0.0).astype(BF16)
    ks_ref[...] = pad_keys
    kw_ref[...] = pad_keys
    vs_ref[...] = jnp.zeros(vs_ref.shape, BF16)
    vw_ref[...] = jnp.zeros(vw_ref.shape, BF16)


def _pad_tiles(ks, kw, vst, vwt):
    B = ks.shape[0]
    last = (ks.shape[2] - KTILE) // KTILE
    anyspec = pl.BlockSpec(memory_space=pl.ANY)
    keys = pl.BlockSpec((1, N_KV, KTILE, LANES), lambda b: (b, 0, last, 0))
    vals = pl.BlockSpec((1, N_KV, VROWS, KTILE), lambda b: (b, 0, 0, last))
    return pl.pallas_call(
        _pad_tiles_kernel,
        grid=(B,),
        in_specs=[anyspec] * 4,
        out_specs=[keys, keys, vals, vals],
        out_shape=[jax.ShapeDtypeStruct(a.shape, a.dtype) for a in (ks, kw, vst, vwt)],
        input_output_aliases={0: 0, 1: 1, 2: 2, 3: 3},
        compiler_params=_params("parallel"),
        name="pad_tiles",
    )(ks, kw, vst, vwt)


def _in_proj(x, g, w_a, w_bt):
    B, S, _ = x.shape
    tm = ROW_TILE
    nq = tm // QBLK
    full = lambda a: pl.BlockSpec(a.shape, lambda b, s: (0,) * a.ndim)
    seq = lambda n: pl.BlockSpec((1, tm, n), lambda b, s: (b, s, 0))
    keys = pl.BlockSpec((1, N_KV, tm, LANES), lambda b, s: (b, 0, s, 0))
    vals = pl.BlockSpec((1, N_KV, VROWS, tm), lambda b, s: (b, 0, 0, s))
    return pl.pallas_call(
        _in_proj_kernel,
        grid=(B, S // tm),
        in_specs=[seq(D_MODEL), full(g), full(w_a), full(w_bt)],
        out_specs=[
            seq(POOL_WIDTH), seq(2 * CONV_WIDTH),
            pl.BlockSpec((1, 2, N_KV, tm, HEAD_DIM), lambda b, s: (b, 0, 0, s, 0)),
            keys, keys,
            pl.BlockSpec((1, N_KV, nq, HEAD_DIM, QROWS), lambda b, s: (b, 0, s, 0, 0)),
            vals, vals,
            pl.BlockSpec((1, nq, N_KV, Q_PER_KV * N_BRANCH, QBLK), lambda b, s: (b, s, 0, 0, 0)),
        ],
        out_shape=[
            jax.ShapeDtypeStruct((B, S, POOL_WIDTH), F32),
            jax.ShapeDtypeStruct((B, S, 2 * CONV_WIDTH), F32),
            jax.ShapeDtypeStruct((B, 2, N_KV, S, HEAD_DIM), F32),
            jax.ShapeDtypeStruct((B, N_KV, S + KTILE, LANES), BF16),
            jax.ShapeDtypeStruct((B, N_KV, S + KTILE, LANES), BF16),
            jax.ShapeDtypeStruct((B, N_KV, S // QBLK, HEAD_DIM, QROWS), BF16),
            jax.ShapeDtypeStruct((B, N_KV, VROWS, S + KTILE), BF16),
            jax.ShapeDtypeStruct((B, N_KV, VROWS, S + KTILE), BF16),
            jax.ShapeDtypeStruct((B, S // QBLK, N_KV, Q_PER_KV * N_BRANCH, QBLK), F32),
        ],
        compiler_params=_params("parallel", "parallel"),
        name="in_proj",
    )(x, g, w_a, w_bt)


def _mixers_kernel(up_ref, uc_ref, pw_ref, ps_ref, cw_ref, cb_ref, lg_ref, lb_ref, cpw_ref,
                   yp_ref, yc_ref, ubuf, hbuf):
    s = pl.program_id(1)
    ts = up_ref.shape[1]

    @pl.when(s == 0)
    def _():
        ubuf[0:HALO, :] = jnp.zeros((HALO, POOL_WIDTH), F32)
        hbuf[0:HALO, :] = jnp.zeros((HALO, CONV_WIDTH), F32)

    ext = HALO + ts
    back = lambda a, k: a + pltpu.roll(a, k, 0)

    u = up_ref[0]
    ubuf[HALO:ext, :] = u
    xl = ubuf[:, 0:128]
    xh = ubuf[:, 128:256]
    s2l = back(xl, 1)
    s4l = back(s2l, 2)
    s8h = back(back(back(xh, 1), 2), 4)
    s16h = back(s8h, 8)
    t1 = (s * ts + lax.broadcasted_iota(jnp.int32, (ts, 128), 0) + 1).astype(F32)
    lane = lax.broadcasted_iota(jnp.int32, (ts, 128), 1)
    first = lane < POOL_GROUP
    sum_lo = jnp.where(first, s2l[HALO:ext], s4l[HALO:ext])
    den_lo = jnp.minimum(t1, jnp.where(first, 2.0, 4.0))
    sum_hi = jnp.where(first, s8h[HALO:ext], s16h[HALO:ext])
    den_hi = jnp.minimum(t1, jnp.where(first, 8.0, 16.0))
    d = jnp.concatenate([sum_lo / den_lo - u[:, 0:128], sum_hi / den_hi - u[:, 128:256]], axis=1)
    yp_ref[0] = (_dot(d.astype(BF16), pw_ref[...]) * ps_ref[...]).astype(BF16)
    ubuf[0:HALO, :] = ubuf[ts:ext, :]

    uc = uc_ref[0]
    hbuf[HALO:ext, :] = uc[:, 0:CONV_WIDTH] * jax.nn.sigmoid(uc[:, CONV_WIDTH:])
    halves = []
    for c0 in range(0, CONV_WIDTH, LANES):
        hx = hbuf[:, c0:c0 + LANES]
        ahead = [hx] + [pltpu.roll(hx, ext - r, 0) for r in range(1, 8)]
        acc_h = jnp.zeros((ts, LANES), F32) + cb_ref[:, c0:c0 + LANES]
        for j in range(CONV_KSIZE):
            a8, r = divmod(HALO - (CONV_KSIZE - 1) + j, 8)
            acc_h = acc_h + ahead[r][8 * a8:8 * a8 + ts] * cw_ref[j:j + 1, c0:c0 + LANES]
        halves.append(acc_h)
    acc = jnp.concatenate(halves, axis=1)
    mu = jnp.mean(acc, axis=-1, keepdims=True)
    var = jnp.mean(jnp.square(acc - mu), axis=-1, keepdims=True)
    y = (acc - mu) * lax.rsqrt(var + EPS) * lg_ref[...] + lb_ref[...]
    y = y * jax.nn.sigmoid(y)
    yc_ref[0] = _dot(y.astype(BF16), cpw_ref[...]).astype(BF16)
    hbuf[0:HALO, :] = hbuf[ts:ts + HALO, :]


def _mixers(up, uc, pw, ps, cw, cb, lg, lb, cpw):
    B, S, _ = up.shape
    ts = ROW_TILE
    seq = lambda n: pl.BlockSpec((1, ts, n), lambda b, s: (b, s, 0))
    full = lambda a: pl.BlockSpec(a.shape, lambda b, s: (0,) * a.ndim)
    return pl.pallas_call(
        _mixers_kernel,
        grid=(B, S // ts),
        in_specs=[seq(256), seq(512)] + [full(a) for a in (pw, ps, cw, cb, lg, lb, cpw)],
        out_specs=[seq(256), seq(256)],
        out_shape=[jax.ShapeDtypeStruct((B, S, 256), BF16)] * 2,
        scratch_shapes=[pltpu.VMEM((HALO + ts, POOL_WIDTH), F32), pltpu.VMEM((HALO + ts, CONV_WIDTH), F32)],
        compiler_params=_params("parallel", "arbitrary"),
        name="mixers",
    )(up, uc, pw, ps, cw, cb, lg, lb, cpw)


def _compress_kernel(r_ref, pos_ref, w1_ref, w2k_ref, w2vt_ref, kc_ref, vct_ref):
    ncp = r_ref.shape[3] // CMP_STRIDE
    half = CMP_STRIDE * HEAD_DIM

    def hidden(kv):
        r = jnp.concatenate([r_ref[0, kv, 0, pl.ds(l, ncp, stride=CMP_STRIDE), :] for l in range(CMP_STRIDE)],
                            axis=1)
        a = _dot((r + pos_ref[kv, 0]).astype(BF16), w1_ref[kv, 0:half, :])
        b = _dot((r + pos_ref[kv, 1]).astype(BF16), w1_ref[kv, half:2 * half, :])
        return jax.nn.gelu(a + pltpu.roll(b, ncp - 1, 0)).astype(BF16)

    kc = _dot(hidden(0), w2k_ref[...]).astype(BF16)
    kc_ref[0, 0] = jnp.concatenate([kc, jnp.zeros((ncp, LANES - HEAD_DIM), BF16)], axis=1)
    ones_row = jnp.where(lax.broadcasted_iota(jnp.int32, (VROWS - HEAD_DIM, ncp), 0) == 0, 1.0, 0.0).astype(BF16)
    vct_ref[0, 0] = jnp.concatenate([_dot_nt(w2vt_ref[...], hidden(1)).astype(BF16), ones_row], axis=0)


def _compress(r, pos, w1, w2k, w2vt):
    B, _, G, S, _ = r.shape
    ncp = S // CMP_STRIDE
    full = lambda a: pl.BlockSpec(a.shape, lambda b, g: (0,) * a.ndim)
    return pl.pallas_call(
        _compress_kernel,
        grid=(B, G),
        in_specs=[pl.BlockSpec((1, 2, 1, S, HEAD_DIM), lambda b, g: (b, 0, g, 0, 0)),
                  full(pos), full(w1), full(w2k), full(w2vt)],
        out_specs=[pl.BlockSpec((1, 1, ncp, LANES), lambda b, g: (b, g, 0, 0)),
                   pl.BlockSpec((1, 1, VROWS, ncp), lambda b, g: (b, g, 0, 0))],
        out_shape=[jax.ShapeDtypeStruct((B, G, ncp, LANES), BF16),
                   jax.ShapeDtypeStruct((B, G, VROWS, ncp), BF16)],
        compiler_params=_params("parallel", "parallel"),
        name="compress",
    )(r, pos, w1, w2k, w2vt)


def _rel_bucket_np(dist):
    n = np.maximum(dist, 0)
    nf = np.maximum(n, 1).astype(np.float64)
    large = MAX_EXACT + (np.log(nf / MAX_EXACT) / math.log(MAX_DISTANCE / MAX_EXACT)
                         * (N_BUCKETS - MAX_EXACT)).astype(np.int64)
    return np.where(n < MAX_EXACT, n, np.minimum(large, N_BUCKETS - 1)).astype(np.int32)


def _static_maps():
    qq = np.arange(QBLK)[:, None]
    kk = np.arange(WINDOW + QBLK)[None, :]
    dist = qq + WINDOW - kk
    wmap = np.where((dist >= 0) & (dist < WINDOW), _rel_bucket_np(dist), -1)
    u = np.arange(LANES)[None, :]
    dist_c = qq - CMP_STRIDE * (u - CNEAR_LO) - (CMP_LEN - 1)
    cmap = np.where((dist_c >= 0) & (u < CNEAR), _rel_bucket_np(dist_c), -1)
    cmap = np.where(u < CNEAR, cmap, -2)
    return wmap.astype(np.int32), cmap.astype(np.int32)


def _tables_kernel(bias_ref, wmap_ref, cmap_ref, tw_ref, tc_ref):
    h = pl.program_id(0)
    far = bias_ref[N_BUCKETS - 1, h]

    def lookup(m):
        out = jnp.where(m == -1, NEG_INF, 0.0).astype(F32)
        for b in range(N_BUCKETS):
            out = jnp.where(m == b, (bias_ref[b, h] - far) * LOG2E, out)
        return out

    tw_ref[0] = lookup(wmap_ref[...])
    cmap = cmap_ref[...]
    val = lookup(cmap)
    hi = val.astype(BF16)
    lo = (val - hi.astype(F32)).astype(BF16)
    col = lax.broadcasted_iota(jnp.int32, cmap.shape, 1)
    hi_part = jnp.where(col < CNEAR, hi.astype(F32), 0.0)
    lo_part = pltpu.roll(jnp.where(col < CNEAR, lo.astype(F32), 0.0), CNEAR, 1)
    one = jnp.where(col == 2 * CNEAR, 1.0, 0.0)
    tc_ref[0] = (hi_part + lo_part + one).astype(BF16)


def _bias_tables(rel_bias):
    wmap, cmap = _static_maps()
    nw = WINDOW + QBLK
    return pl.pallas_call(
        _tables_kernel,
        grid=(N_HEADS,),
        in_specs=[
            pl.BlockSpec(memory_space=pltpu.SMEM),
            pl.BlockSpec((QBLK, nw), lambda h: (0, 0)),
            pl.BlockSpec((QBLK, LANES), lambda h: (0, 0)),
        ],
        out_specs=[
            pl.BlockSpec((1, QBLK, nw), lambda h: (h, 0, 0)),
            pl.BlockSpec((1, QBLK, LANES), lambda h: (h, 0, 0)),
        ],
        out_shape=[
            jax.ShapeDtypeStruct((N_HEADS, QBLK, nw), F32),
            jax.ShapeDtypeStruct((N_HEADS, QBLK, LANES), BF16),
        ],
        compiler_params=_params("parallel"),
        name="bias_tables",
    )(rel_bias, jnp.asarray(wmap), jnp.asarray(cmap))


def _nsa_tables(S):
    ncp = S // CMP_STRIDE
    nb = LANES
    pos = np.arange(S + KTILE)
    eblk = (pos[:, None] // SEL_BLOCK == np.arange(nb)[None, :]) & (pos[:, None] < S)
    cs = np.arange(ncp)[None, :] * CMP_STRIDE
    ss = np.arange(nb)[:, None] * SEL_BLOCK
    ovlt = (cs < ss + SEL_BLOCK) & (cs + CMP_LEN > ss) & (np.arange(ncp)[None, :] < ncp - 1) & (ss < S)
    rel = np.arange(2 * ncp)[:, None] - ncp + CNEAR_LO
    u = np.arange(LANES)[None, :]
    place = np.where((rel == u) & (u < CNEAR), 1.0, 0.0)
    place = np.where((rel == u - CNEAR) & (u >= CNEAR) & (u < 2 * CNEAR), 1.0, place)
    place = np.where((u == 2 * CNEAR) & (rel >= CNEAR), NEG_BF16, place)
    return jnp.asarray(eblk, BF16), jnp.asarray(ovlt, BF16), jnp.asarray(place, BF16)


def _nsa_kernel(qt_ref, kc_ref, vc_ref, eb_ref, ks_ref, vs_ref, kw_ref, vw_ref, gate_ref, tw_ref, tc_ref,
                ovl_ref, place_ref, o_ref, qaug_ref, s0_ref, s1_ref, m_ref, acc_ref, oc_next_ref, msel_next_ref):
    i = pl.program_id(2)
    ncp = kc_ref.shape[2]
    nb = ovl_ref.shape[0]
    pad_tile = ks_ref.shape[2] // KTILE - 1
    col_max = lambda a: jnp.max(a, axis=0, keepdims=True)
    prob = lambda a: jnp.exp2(a).astype(BF16)
    tile_off = lambda t, ok: pl.multiple_of(jnp.where(ok, t, pad_tile) * KTILE, KTILE)

    def query_operand(q_blk):
        row = lax.broadcasted_iota(jnp.int32, (LANES - HEAD_DIM, QROWS), 0)
        return jnp.concatenate([q_blk, jnp.where(row == 0, NEG_BF16, 0.0).astype(BF16)], axis=0)

    def select_operands(q_blk, blk):
        place = place_ref[pl.ds(pl.multiple_of(ncp - CMP_STRIDE * blk, CMP_STRIDE), ncp), :]
        return place, jnp.concatenate([query_operand(q_blk), tc_ref[0]], axis=0)

    def cmp_logits(place, rhs, c0, rows):
        return _dot(jnp.concatenate([kc_ref[0, 0, c0:c0 + rows, :], place[c0:c0 + rows]], axis=1), rhs)

    def cmp_values(c0, rows):
        return jnp.concatenate([vc_ref[0, 0, :, c0:c0 + rows], ovl_ref[:, c0:c0 + rows]], axis=0)

    def stage(state, s, vt):
        if state is None:
            m = jnp.maximum(col_max(s), M_INIT)
            return m, _dot(vt, prob(s - m))
        m_prev, acc = state
        m = jnp.maximum(m_prev, col_max(s))
        return m, jnp.exp2(m_prev - m) * acc + _dot(vt, prob(s - m))

    def select_store(acc, blk):
        inv_l = 1.0 / jnp.maximum(acc[HEAD_DIM:HEAD_DIM + 1, :], 1e-30)
        imp4 = acc[VROWS:VROWS + nb, :] * inv_l
        imp = imp4[:, 0:QBLK] + imp4[:, QBLK:2 * QBLK] + imp4[:, 2 * QBLK:3 * QBLK] + imp4[:, 3 * QBLK:4 * QBLK]
        jb = lax.broadcasted_iota(jnp.int32, (nb, QBLK), 0)
        qq = lax.broadcasted_iota(jnp.int32, (nb, QBLK), 1)
        cur = (QBLK // SEL_BLOCK) * blk + (qq >> 6)
        forced = (jb == 0) | (jb == cur) | (jb == cur - 1)
        sc = jnp.where(forced, TAKEN, jnp.where(jb <= cur, imp, NEG_INF))
        jio = jb.astype(F32)
        for _ in range(N_SELECT - N_FORCED):
            idx = jnp.min(jnp.where(sc == col_max(sc), jio, float(nb)), axis=0, keepdims=True)
            sc = jnp.where(jio == idx, TAKEN, sc)
        msel = jnp.where(sc == TAKEN, 0.0, NEG_BF16).astype(BF16)
        oc_next_ref[...] = acc[0:HEAD_DIM, :] * inv_l
        msel_next_ref[...] = jnp.concatenate([msel] * Q_PER_KV, axis=1)

    @pl.when(i == 0)
    def _():
        place, rhs = select_operands(qt_ref[0, 0, 0], 0)
        select_store(stage(None, cmp_logits(place, rhs, 0, ncp), cmp_values(0, ncp))[1], 0)

    qn = query_operand(qt_ref[0, 0, i])
    qaug_ref[nb:nb + LANES, :] = qn
    qaug_ref[0:nb, :] = msel_next_ref[...]
    o_c = oc_next_ref[...]

    def sel_logits(o):
        keys = jnp.concatenate([eb_ref[pl.ds(o, KTILE), :], ks_ref[0, 0, pl.ds(o, KTILE), :]], axis=1)
        return _dot(keys, qaug_ref[...])

    n_far = jnp.maximum(i - 1, 0)
    far_off = lambda t: tile_off(t, t < n_far)

    nxt = jnp.minimum(i + 1, pl.num_programs(2) - 1)
    place_n, rhs_n = select_operands(qt_ref[0, 0, nxt], nxt)
    ct = min(KTILE, ncp)
    jobs = [("cmp", ct, functools.partial(cmp_logits, place_n, rhs_n, c0, ct), functools.partial(cmp_values, c0, ct))
            for c0 in range(0, ncp, ct)]
    for tt in range(WINDOW // KTILE + 1):
        t = i - WINDOW // KTILE + tt
        o = tile_off(t, t >= 0)
        jobs.append(("win", KTILE,
                     lambda o=o, tt=tt: (_dot(kw_ref[0, 0, pl.ds(o, KTILE), :], qn)
                                         + tw_ref[0, tt * KTILE:(tt + 1) * KTILE, :]),
                     lambda o=o: vw_ref[0, 0, :, pl.ds(o, KTILE)]))
    for o, t0 in ((tile_off(i - 1, i >= 1), WINDOW - KTILE), (pl.multiple_of(i * KTILE, KTILE), WINDOW)):
        jobs.append(("sel", KTILE, lambda o=o, t0=t0: sel_logits(o) + tw_ref[0, t0:t0 + KTILE, :],
                     lambda o=o: vs_ref[0, 0, :, pl.ds(o, KTILE)]))
    jobs.append(("far", KTILE, lambda: sel_logits(far_off(0)), None))

    bufs = (s0_ref, s1_ref)
    bufs[0][0:jobs[0][1], :] = jobs[0][2]()
    state = {}
    for k, (branch, rows, _, values) in enumerate(jobs[:-1]):
        bufs[(k + 1) % 2][0:jobs[k + 1][1], :] = jobs[k + 1][2]()
        state[branch] = stage(state.get(branch), bufs[k % 2][0:rows, :], values())
    far_bufs = (bufs[(len(jobs) - 1) % 2], bufs[len(jobs) % 2])
    acc_w = state["win"][1]
    m_ref[...] = jnp.broadcast_to(state["sel"][0], m_ref.shape)
    acc_ref[...] = state["sel"][1]
    select_store(state["cmp"][1], nxt)

    def update(s_ref, t):
        o = far_off(t)
        sl = s_ref[...]
        m_prev = m_ref[...]
        m_next = jnp.maximum(m_prev, col_max(sl))
        alpha = jnp.exp2(m_prev - m_next)
        pt = prob(sl - m_next[0:1, :])
        acc_ref[...] = alpha[0:1, :] * acc_ref[...] + _dot(vs_ref[0, 0, :, pl.ds(o, KTILE)], pt)
        m_ref[...] = m_next

    def far_trips(first_tile, n_trips, unroll):
        def body(k, carry):
            for h in range(unroll):
                t = first_tile + unroll * k + h
                far_bufs[(h + 1) % 2][...] = sel_logits(far_off(t + 1))
                update(far_bufs[h % 2], t)
            return carry

        lax.fori_loop(0, n_trips, body, 0)

    done = 0
    for unroll in FAR_UNROLLS[:-1]:
        n_trips = (n_far - done) // unroll
        far_trips(done, n_trips, unroll)
        done = done + n_trips * unroll
    last = FAR_UNROLLS[-1]
    far_trips(done, (n_far - done + last - 1) // last, last)
    acc_s = acc_ref[...]

    gb = gate_ref[0, i, 0]
    gate = lambda br: jnp.concatenate(
        [gb[r * N_BRANCH + br:r * N_BRANCH + br + 1, :] for r in range(Q_PER_KV)], axis=1)
    out = (gate(0) * o_c
           + (gate(1) / acc_s[HEAD_DIM:HEAD_DIM + 1, :]) * acc_s[0:HEAD_DIM, :]
           + (gate(2) / acc_w[HEAD_DIM:HEAD_DIM + 1, :]) * acc_w[0:HEAD_DIM, :]).astype(BF16)
    q_off = pl.multiple_of(i * QBLK, QBLK)
    for r in range(Q_PER_KV):
        o_ref[0, r * HEAD_DIM:(r + 1) * HEAD_DIM, pl.ds(q_off, QBLK)] = out[:, r * QBLK:(r + 1) * QBLK]


def _nsa(qt, kc, vct, eblk, ks, vst, kw, vwt, gates, twt, tct, ovlt, place):
    B, G, NQ, _, _ = qt.shape
    S = NQ * QBLK
    nb = ovlt.shape[0]
    per_bg = lambda a: pl.BlockSpec((1, 1) + a.shape[2:], lambda b, g, i: (b, g) + (0,) * (a.ndim - 2))
    per_g = lambda a: pl.BlockSpec((1,) + a.shape[1:], lambda b, g, i: (g,) + (0,) * (a.ndim - 1))
    const = lambda a: pl.BlockSpec(a.shape, lambda b, g, i: (0,) * a.ndim)
    return pl.pallas_call(
        _nsa_kernel,
        grid=(B, G, NQ),
        in_specs=[per_bg(qt),
                  per_bg(kc), per_bg(vct), const(eblk), per_bg(ks), per_bg(vst), per_bg(kw), per_bg(vwt),
                  pl.BlockSpec((1, NQ, 1, Q_PER_KV * N_BRANCH, QBLK), lambda b, g, i: (b, 0, g, 0, 0)),
                  per_g(twt), per_g(tct), const(ovlt), const(place)],
        out_specs=pl.BlockSpec((1, Q_PER_KV * HEAD_DIM, S), lambda b, g, i: (b, g, 0)),
        out_shape=jax.ShapeDtypeStruct((B, NSA_WIDTH, S), BF16),
        scratch_shapes=[pltpu.VMEM((nb + LANES, QROWS), BF16),
                        pltpu.VMEM((KTILE, QROWS), F32), pltpu.VMEM((KTILE, QROWS), F32),
                        pltpu.VMEM((8, QROWS), F32), pltpu.VMEM((VROWS, QROWS), F32),
                        pltpu.VMEM((HEAD_DIM, QROWS), F32), pltpu.VMEM((nb, QROWS), BF16)],
        compiler_params=_params("parallel", "parallel", "arbitrary"),
        name="nsa",
    )(qt, kc, vct, eblk, ks, vst, kw, vwt, gates, twt, tct, ovlt, place)


def _mem_kv_kernel(mem_ref, g_ref, wk_ref, wv_ref, k_ref, v_ref):
    m = _rms(mem_ref[0], g_ref[...]).astype(BF16)
    k_ref[0] = _dot(m, wk_ref[...]).astype(BF16)
    v_ref[0] = _dot(m, wv_ref[...]).astype(BF16)


def _mem_kv(mem, g, wk, wv):
    B, M, _ = mem.shape
    full = lambda a: pl.BlockSpec(a.shape, lambda b: (0,) * a.ndim)
    per_b = pl.BlockSpec((1, M, D_MODEL), lambda b: (b, 0, 0))
    return pl.pallas_call(
        _mem_kv_kernel,
        grid=(B,),
        in_specs=[per_b, full(g), full(wk), full(wv)],
        out_specs=[per_b, per_b],
        out_shape=[jax.ShapeDtypeStruct((B, M, D_MODEL), BF16)] * 2,
        compiler_params=_params("parallel"),
        name="mem_kv",
    )(mem, g, wk, wv)


def _mix_xattn_kernel(x_ref, yp_ref, yc_ref, ynt_ref, wmix_ref, gmix_ref,
                      gpre_ref, wq_ref, k_ref, v_ref, wo_ref, gpost_ref, o_ref):
    ts = x_ref.shape[1]
    subs = [slice(r, r + XA_SUB) for r in range(0, ts, XA_SUB)]
    mix = [_dot(yp_ref[0, r, :], wmix_ref[0:256, :]) + _dot(yc_ref[0, r, :], wmix_ref[256:512, :])
           + lax.dot_general(ynt_ref[0, :, r], wmix_ref[512:1024, :], (((0,), (0,)), ((), ())),
                             preferred_element_type=F32) for r in subs]
    x = [x_ref[0, r, :] + _rms(m, gmix_ref[...]) for r, m in zip(subs, mix)]
    h = [_rms(xr, gpre_ref[...]).astype(BF16) for xr in x]
    qx = [(_dot(hr, wq_ref[...]) * (XA_HEAD_DIM ** -0.5)).astype(BF16) for hr in h]
    attn = []
    for qr in qx:
        outs = []
        for hh in range(XA_HEADS):
            sl = slice(hh * XA_HEAD_DIM, (hh + 1) * XA_HEAD_DIM)
            s = _dot_nt(qr[:, sl], k_ref[0, :, sl])
            p = jnp.exp(s - jnp.max(s, axis=1, keepdims=True))
            p = p * (1.0 / jnp.sum(p, axis=1, keepdims=True))
            outs.append(_dot(p.astype(BF16), v_ref[0, :, sl]).astype(BF16))
        attn.append(jnp.concatenate(outs, axis=1))
    for r, xr, o in zip(subs, x, attn):
        o_ref[0, r, :] = xr + _rms(_dot(o, wo_ref[...]), gpost_ref[...])


def _mix_xattn(x, yp, yc, ynt, wmix, gmix, gpre, wq, km, vm, wo, gpost):
    B, S, _ = x.shape
    M = km.shape[1]
    ts = XA_TILE
    full = lambda a: pl.BlockSpec(a.shape, lambda b, s: (0,) * a.ndim)
    seq = lambda n: pl.BlockSpec((1, ts, n), lambda b, s: (b, s, 0))
    per_b = pl.BlockSpec((1, M, D_MODEL), lambda b, s: (b, 0, 0))
    return pl.pallas_call(
        _mix_xattn_kernel,
        grid=(B, S // ts),
        in_specs=[seq(D_MODEL), seq(POOL_WIDTH), seq(CONV_WIDTH),
                  pl.BlockSpec((1, NSA_WIDTH, ts), lambda b, s: (b, 0, s)), full(wmix), full(gmix),
                  full(gpre), full(wq), per_b, per_b, full(wo), full(gpost)],
        out_specs=seq(D_MODEL),
        out_shape=jax.ShapeDtypeStruct((B, S, D_MODEL), F32),
        compiler_params=_params("parallel", "parallel"),
        name="mix_xattn",
    )(x, yp, yc, ynt, wmix, gmix, gpre, wq, km, vm, wo, gpost)


def _mlp_kernel(x_ref, gpre_ref, w1_ref, w2_ref, gpost_ref, o_ref):
    x = x_ref[...]
    h = _rms(x, gpre_ref[...]).astype(BF16)
    y = jnp.zeros(x.shape, F32)
    chunk = 1024
    for c in range(D_FF // chunk):
        a = jnp.maximum(_dot(h, w1_ref[:, c * chunk:(c + 1) * chunk]), 0.0)
        y = y + _dot((a * a).astype(BF16), w2_ref[c * chunk:(c + 1) * chunk, :])
    o_ref[...] = x + _rms(y, gpost_ref[...])


def _mlp(xf, gpre, w1, w2, gpost):
    T = xf.shape[0]
    row = pl.BlockSpec((ROW_TILE, D_MODEL), lambda i: (i, 0))
    full = lambda a: pl.BlockSpec(a.shape, lambda i: (0,) * a.ndim)
    return pl.pallas_call(
        _mlp_kernel,
        grid=(T // ROW_TILE,),
        in_specs=[row, full(gpre), full(w1), full(w2), full(gpost)],
        out_specs=row,
        out_shape=jax.ShapeDtypeStruct((T, D_MODEL), F32),
        compiler_params=_params("parallel"),
        name="mlp",
    )(xf, gpre, w1, w2, gpost)


def kernel(x, mem, rel_bias, mix_pre_g, mix_post_g, w_in, pool_w, pool_scale, conv_w, conv_b, conv_ln_g,
           conv_ln_b, conv_pw, cmp_k_pos, cmp_k_w1, cmp_k_w2, cmp_v_pos, cmp_v_w1, cmp_v_w2, w_out,
           xa_pre_g, xa_post_g, mem_g, xa_wq, xa_wk, xa_wv, xa_wo, mlp_pre_g, mlp_post_g, mlp_w1, mlp_w2):
    B, S, _ = x.shape
    assert S % ROW_TILE == 0 and S // SEL_BLOCK >= N_SELECT and S // SEL_BLOCK <= LANES
    T = B * S
    ncp = S // CMP_STRIDE
    row2 = lambda a: a.reshape(1, -1)

    tw, tc = _bias_tables(rel_bias)
    twt = tw.reshape(N_KV, QROWS, WINDOW + QBLK).transpose(0, 2, 1)
    tct = tc.reshape(N_KV, QROWS, LANES).transpose(0, 2, 1)
    eblk, ovlt, place = _nsa_tables(S)

    for l in range(DEPTH):
        w_a, w_bt = _in_proj_weights(w_in[l])
        up, uc, kcr, ks, kw, qt, vst, vwt, gates = _in_proj(x, row2(mix_pre_g[l]), w_a, w_bt)
        ks, kw, vst, vwt = _pad_tiles(ks, kw, vst, vwt)

        pw_bd = jnp.zeros((POOL_WIDTH, POOL_WIDTH), F32)
        for g in range(len(POOL_WINDOWS)):
            pw_bd = lax.dynamic_update_slice(pw_bd, pool_w[l, g], (g * POOL_GROUP, g * POOL_GROUP))
        yp, yc = _mixers(up, uc, pw_bd.astype(BF16), row2(pool_scale[l]), conv_w[l], row2(conv_b[l]),
                         row2(conv_ln_g[l]), row2(conv_ln_b[l]), conv_pw[l].astype(BF16))

        cpos = jnp.stack([cmp_k_pos[l], cmp_v_pos[l]]).reshape(2, 2, 1, CMP_STRIDE * HEAD_DIM)
        cw1 = jnp.stack([cmp_k_w1[l], cmp_v_w1[l]]).astype(BF16)
        kc, vct = _compress(kcr, cpos, cw1, cmp_k_w2[l].astype(BF16), cmp_v_w2[l].T.astype(BF16))

        ynt = _nsa(qt, kc, vct, eblk, ks, vst, kw, vwt, gates, twt, tct, ovlt, place)

        km, vm = _mem_kv(mem, row2(mem_g[l]), xa_wk[l].astype(BF16), xa_wv[l].astype(BF16))
        x = _mix_xattn(x, yp, yc, ynt, w_out[l].astype(BF16), row2(mix_post_g[l]), row2(xa_pre_g[l]),
                       xa_wq[l].astype(BF16), km, vm, xa_wo[l].astype(BF16), row2(xa_post_g[l]))

        x = _mlp(x.reshape(T, D_MODEL), row2(mlp_pre_g[l]), mlp_w1[l].astype(BF16), mlp_w2[l].astype(BF16),
                 row2(mlp_post_g[l])).reshape(B, S, D_MODEL)
    return x
```

```python
import functools
import math

import numpy as np
import jax
import jax.numpy as jnp
from jax import lax
from jax.experimental import pallas as pl
from jax.experimental.pallas import tpu as pltpu

F32 = jnp.float32
BF16 = jnp.bfloat16

D_MODEL = 1024
DEPTH = 2
POOL_WIDTH = 256
POOL_WINDOWS = (2, 4, 8, 16)
POOL_GROUP = 64
CONV_WIDTH = 256
CONV_KSIZE = 31
NSA_WIDTH = 512
HEAD_DIM = 64
N_HEADS = 8
N_KV = 2
Q_PER_KV = 4
CMP_LEN = 32
CMP_STRIDE = 16
CMP_HIDDEN = 256
SEL_BLOCK = 64
N_SELECT = 16
WINDOW = 512
N_BRANCH = 3
N_BUCKETS = 32
MAX_EXACT = 16
MAX_DISTANCE = 128
XA_HEADS = 4
XA_HEAD_DIM = 256
D_FF = 4096
EPS = 1e-6
NEG_INF = -1e30
FORCE = 1e30

LANES = 128
ROW_TILE = 512
XA_SUB = 512
XA_TILE = 1024
QBLK = 256
KTILE = 256
QROWS = Q_PER_KV * QBLK
HALO = 32
CNEAR = 24
CNEAR_LO = 8
NEG_BF16 = -(2.0 ** 100)
M_INIT = -1e29
TAKEN = -3e38
N_FORCED = 3
LOG2E = 1.4426950408889634
FAR_UNROLLS = (8, 4, 2, 1)
VROWS = 80
VMEM_LIMIT = 56 * 1024 * 1024


def _rms(x, g):
    return x * lax.rsqrt(jnp.mean(x * x, axis=-1, keepdims=True) + EPS) * g


def _dot(a, b):
    return jnp.dot(a, b, preferred_element_type=F32)


def _dot_nt(a, b):
    return lax.dot_general(a, b, (((1,), (1,)), ((), ())), preferred_element_type=F32)


def _params(*sem):
    return pltpu.CompilerParams(dimension_semantics=sem, vmem_limit_bytes=VMEM_LIMIT)


def _in_proj_weights(w):
    z4 = jnp.zeros((D_MODEL, 4), w.dtype)
    kv0 = POOL_WIDTH + 2 * CONV_WIDTH + NSA_WIDTH
    w_a = jnp.concatenate([w[:, 0:768], w[:, kv0 + 256:kv0 + 384], w[:, kv0 + 512:kv0 + 640],
                           w[:, kv0:kv0 + 256]], axis=1)
    gw = w[:, kv0 + 768:]
    w_b = jnp.concatenate([w[:, 768:1280], w[:, kv0 + 384:kv0 + 512], w[:, kv0 + 640:kv0 + 768],
                           gw[:, 0:12], z4, gw[:, 12:24], z4], axis=1)
    return w_a.astype(BF16), w_b.T.astype(BF16)


def _in_proj_kernel(x_ref, g_ref, wa_ref, wbt_ref,
                    pool_ref, conv_ref, kc_ref, ks_ref, kw_ref, qt_ref, vs_ref, vw_ref, gate_ref):
    tm = x_ref.shape[1]
    h = _rms(x_ref[0], g_ref[...]).astype(BF16)
    pool_ref[0] = _dot(h, wa_ref[:, 0:256])
    conv_ref[0] = _dot(h, wa_ref[:, 256:768])
    ksw = _dot(h, wa_ref[:, 768:1024]).astype(BF16)
    kvc = _dot(h, wa_ref[:, 1024:1280])
    lane_pad = jnp.zeros((tm, LANES - HEAD_DIM), BF16)
    for g in range(N_KV):
        ks_ref[0, g] = jnp.concatenate([ksw[:, g * HEAD_DIM:(g + 1) * HEAD_DIM], lane_pad], axis=1)
        kw_ref[0, g] = jnp.concatenate([ksw[:, (N_KV + g) * HEAD_DIM:(N_KV + g + 1) * HEAD_DIM], lane_pad], axis=1)
        for kv in range(2):
            kc_ref[0, kv, g] = kvc[:, (kv * N_KV + g) * HEAD_DIM:(kv * N_KV + g + 1) * HEAD_DIM]

    pt = _dot_nt(wbt_ref[...], h)
    qs = (pt[0:NSA_WIDTH] * (HEAD_DIM ** -0.5 * LOG2E)).astype(BF16)
    ones_row = jnp.where(lax.broadcasted_iota(jnp.int32, (VROWS - HEAD_DIM, tm), 0) == 0, 1.0, 0.0).astype(BF16)
    gs = jax.nn.sigmoid(pt[NSA_WIDTH + 256:NSA_WIDTH + 288])
    for g in range(N_KV):
        for qb in range(tm // QBLK):
            cols = slice(qb * QBLK, (qb + 1) * QBLK)
            qt_ref[0, g, qb] = jnp.concatenate(
                [qs[(g * Q_PER_KV + r) * HEAD_DIM:(g * Q_PER_KV + r + 1) * HEAD_DIM, cols] for r in range(Q_PER_KV)],
                axis=1)
            gate_ref[0, qb, g] = gs[g * 16:g * 16 + Q_PER_KV * N_BRANCH, cols]
        v0 = NSA_WIDTH + g * HEAD_DIM
        vs_ref[0, g] = jnp.concatenate([pt[v0:v0 + HEAD_DIM].astype(BF16), ones_row], axis=0)
        vw_ref[0, g] = jnp.concatenate([pt[v0 + 128:v0 + 128 + HEAD_DIM].astype(BF16), ones_row], axis=0)


def _pad_tiles_kernel(ks_in, kw_in, vs_in, vw_in, ks_ref, kw_ref, vs_ref, vw_ref):
    del ks_in, kw_in, vs_in, vw_in
    lane = lax.broadcasted_iota(jnp.int32, ks_ref.shape, 3)
    pad_keys = jnp.where(lane == HEAD_DIM, 1.0, 0.0).astype(BF16)
    ks_ref[...] = pad_keys
    kw_ref[...] = pad_keys
    vs_ref[...] = jnp.zeros(vs_ref.shape, BF16)
    vw_ref[...] = jnp.zeros(vw_ref.shape, BF16)


def _pad_tiles(ks, kw, vst, vwt):
    B = ks.shape[0]
    last = (ks.shape[2] - KTILE) // KTILE
    anyspec = pl.BlockSpec(memory_space=pl.ANY)
    keys = pl.BlockSpec((1, N_KV, KTILE, LANES), lambda b: (b, 0, last, 0))
    vals = pl.BlockSpec((1, N_KV, VROWS, KTILE), lambda b: (b, 0, 0, last))
    return pl.pallas_call(
        _pad_tiles_kernel,
        grid=(B,),
        in_specs=[anyspec] * 4,
        out_specs=[keys, keys, vals, vals],
        out_shape=[jax.ShapeDtypeStruct(a.shape, a.dtype) for a in (ks, kw, vst, vwt)],
        input_output_aliases={0: 0, 1: 1, 2: 2, 3: 3},
        compiler_params=_params("parallel"),
        name="pad_tiles",
    )(ks, kw, vst, vwt)


def _in_proj(x, g, w_a, w_bt):
    B, S, _ = x.shape
    tm = ROW_TILE
    nq = tm // QBLK
    full = lambda a: pl.BlockSpec(a.shape, lambda b, s: (0,) * a.ndim)
    seq = lambda n: pl.BlockSpec((1, tm, n), lambda b, s: (b, s, 0))
    keys = pl.BlockSpec((1, N_KV, tm, LANES), lambda b, s: (b, 0, s, 0))
    vals = pl.BlockSpec((1, N_KV, VROWS, tm), lambda b, s: (b, 0, 0, s))
    return pl.pallas_call(
        _in_proj_kernel,
        grid=(B, S // tm),
        in_specs=[seq(D_MODEL), full(g), full(w_a), full(w_bt)],
        out_specs=[
            seq(POOL_WIDTH), seq(2 * CONV_WIDTH),
            pl.BlockSpec((1, 2, N_KV, tm, HEAD_DIM), lambda b, s: (b, 0, 0, s, 0)),
            keys, keys,
            pl.BlockSpec((1, N_KV, nq, HEAD_DIM, QROWS), lambda b, s: (b, 0, s, 0, 0)),
            vals, vals,
            pl.BlockSpec((1, nq, N_KV, Q_PER_KV * N_BRANCH, QBLK), lambda b, s: (b, s, 0, 0, 0)),
        ],
        out_shape=[
            jax.ShapeDtypeStruct((B, S, POOL_WIDTH), F32),
            jax.ShapeDtypeStruct((B, S, 2 * CONV_WIDTH), F32),
            jax.ShapeDtypeStruct((B, 2, N_KV, S, HEAD_DIM), F32),
            jax.ShapeDtypeStruct((B, N_KV, S + KTILE, LANES), BF16),
            jax.ShapeDtypeStruct((B, N_KV, S + KTILE, LANES), BF16),
            jax.ShapeDtypeStruct((B, N_KV, S // QBLK, HEAD_DIM, QROWS), BF16),
            jax.ShapeDtypeStruct((B, N_KV, VROWS, S + KTILE), BF16),
            jax.ShapeDtypeStruct((B, N_KV, VROWS, S + KTILE), BF16),
            jax.ShapeDtypeStruct((B, S // QBLK, N_KV, Q_PER_KV * N_BRANCH, QBLK), F32),
        ],
        compiler_params=_params("parallel", "parallel"),
        name="in_proj",
    )(x, g, w_a, w_bt)


def _mixers_kernel(up_ref, uc_ref, pw_ref, ps_ref, cw_ref, cb_ref, lg_ref, lb_ref, cpw_ref,
                   yp_ref, yc_ref, ubuf, hbuf):
    s = pl.program_id(1)
    ts = up_ref.shape[1]

    @pl.when(s == 0)
    def _():
        ubuf[0:HALO, :] = jnp.zeros((HALO, POOL_WIDTH), F32)
        hbuf[0:HALO, :] = jnp.zeros((HALO, CONV_WIDTH), F32)

    ext = HALO + ts
    back = lambda a, k: a + pltpu.roll(a, k, 0)

    u = up_ref[0]
    ubuf[HALO:ext, :] = u
    xl = ubuf[:, 0:128]
    xh = ubuf[:, 128:256]
    s2l = back(xl, 1)
    s4l = back(s2l, 2)
    s8h = back(back(back(xh, 1), 2), 4)
    s16h = back(s8h, 8)
    t1 = (s * ts + lax.broadcasted_iota(jnp.int32, (ts, 128), 0) + 1).astype(F32)
    lane = lax.broadcasted_iota(jnp.int32, (ts, 128), 1)
    first = lane < POOL_GROUP
    sum_lo = jnp.where(first, s2l[HALO:ext], s4l[HALO:ext])
    den_lo = jnp.minimum(t1, jnp.where(first, 2.0, 4.0))
    sum_hi = jnp.where(first, s8h[HALO:ext], s16h[HALO:ext])
    den_hi = jnp.minimum(t1, jnp.where(first, 8.0, 16.0))
    d = jnp.concatenate([sum_lo / den_lo - u[:, 0:128], sum_hi / den_hi - u[:, 128:256]], axis=1)
    yp_ref[0] = (_dot(d.astype(BF16), pw_ref[...]) * ps_ref[...]).astype(BF16)
    ubuf[0:HALO, :] = ubuf[ts:ext, :]

    uc = uc_ref[0]
    hbuf[HALO:ext, :] = uc[:, 0:CONV_WIDTH] * jax.nn.sigmoid(uc[:, CONV_WIDTH:])
    halves = []
    for c0 in range(0, CONV_WIDTH, LANES):
        hx = hbuf[:, c0:c0 + LANES]
        ahead = [hx] + [pltpu.roll(hx, ext - r, 0) for r in range(1, 8)]
        acc_h = jnp.zeros((ts, LANES), F32) + cb_ref[:, c0:c0 + LANES]
        for j in range(CONV_KSIZE):
            a8, r = divmod(HALO - (CONV_KSIZE - 1) + j, 8)
            acc_h = acc_h + ahead[r][8 * a8:8 * a8 + ts] * cw_ref[j:j + 1, c0:c0 + LANES]
        halves.append(acc_h)
    acc = jnp.concatenate(halves, axis=1)
    mu = jnp.mean(acc, axis=-1, keepdims=True)
    var = jnp.mean(jnp.square(acc - mu), axis=-1, keepdims=True)
    y = (acc - mu) * lax.rsqrt(var + EPS) * lg_ref[...] + lb_ref[...]
    y = y * jax.nn.sigmoid(y)
    yc_ref[0] = _dot(y.astype(BF16), cpw_ref[...]).astype(BF16)
    hbuf[0:HALO, :] = hbuf[ts:ts + HALO, :]


def _mixers(up, uc, pw, ps, cw, cb, lg, lb, cpw):
    B, S, _ = up.shape
    ts = ROW_TILE
    seq = lambda n: pl.BlockSpec((1, ts, n), lambda b, s: (b, s, 0))
    full = lambda a: pl.BlockSpec(a.shape, lambda b, s: (0,) * a.ndim)
    return pl.pallas_call(
        _mixers_kernel,
        grid=(B, S // ts),
        in_specs=[seq(256), seq(512)] + [full(a) for a in (pw, ps, cw, cb, lg, lb, cpw)],
        out_specs=[seq(256), seq(256)],
        out_shape=[jax.ShapeDtypeStruct((B, S, 256), BF16)] * 2,
        scratch_shapes=[pltpu.VMEM((HALO + ts, POOL_WIDTH), F32), pltpu.VMEM((HALO + ts, CONV_WIDTH), F32)],
        compiler_params=_params("parallel", "arbitrary"),
        name="mixers",
    )(up, uc, pw, ps, cw, cb, lg, lb, cpw)


def _compress_kernel(r_ref, pos_ref, w1_ref, w2k_ref, w2vt_ref, kc_ref, vct_ref):
    ncp = r_ref.shape[3] // CMP_STRIDE
    half = CMP_STRIDE * HEAD_DIM

    def hidden(kv):
        r = jnp.concatenate([r_ref[0, kv, 0, pl.ds(l, ncp, stride=CMP_STRIDE), :] for l in range(CMP_STRIDE)],
                            axis=1)
        a = _dot((r + pos_ref[kv, 0]).astype(BF16), w1_ref[kv, 0:half, :])
        b = _dot((r + pos_ref[kv, 1]).astype(BF16), w1_ref[kv, half:2 * half, :])
        return jax.nn.gelu(a + pltpu.roll(b, ncp - 1, 0)).astype(BF16)

    kc = _dot(hidden(0), w2k_ref[...]).astype(BF16)
    kc_ref[0, 0] = jnp.concatenate([kc, jnp.zeros((ncp, LANES - HEAD_DIM), BF16)], axis=1)
    ones_row = jnp.where(lax.broadcasted_iota(jnp.int32, (VROWS - HEAD_DIM, ncp), 0) == 0, 1.0, 0.0).astype(BF16)
    vct_ref[0, 0] = jnp.concatenate([_dot_nt(w2vt_ref[...], hidden(1)).astype(BF16), ones_row], axis=0)


def _compress(r, pos, w1, w2k, w2vt):
    B, _, G, S, _ = r.shape
    ncp = S // CMP_STRIDE
    full = lambda a: pl.BlockSpec(a.shape, lambda b, g: (0,) * a.ndim)
    return pl.pallas_call(
        _compress_kernel,
        grid=(B, G),
        in_specs=[pl.BlockSpec((1, 2, 1, S, HEAD_DIM), lambda b, g: (b, 0, g, 0, 0)),
                  full(pos), full(w1), full(w2k), full(w2vt)],
        out_specs=[pl.BlockSpec((1, 1, ncp, LANES), lambda b, g: (b, g, 0, 0)),
                   pl.BlockSpec((1, 1, VROWS, ncp), lambda b, g: (b, g, 0, 0))],
        out_shape=[jax.ShapeDtypeStruct((B, G, ncp, LANES), BF16),
                   jax.ShapeDtypeStruct((B, G, VROWS, ncp), BF16)],
        compiler_params=_params("parallel", "parallel"),
        name="compress",
    )(r, pos, w1, w2k, w2vt)


def _rel_bucket_np(dist):
    n = np.maximum(dist, 0)
    nf = np.maximum(n, 1).astype(np.float64)
    large = MAX_EXACT + (np.log(nf / MAX_EXACT) / math.log(MAX_DISTANCE / MAX_EXACT)
                         * (N_BUCKETS - MAX_EXACT)).astype(np.int64)
    return np.where(n < MAX_EXACT, n, np.minimum(large, N_BUCKETS - 1)).astype(np.int32)


def _static_maps():
    qq = np.arange(QBLK)[:, None]
    kk = np.arange(WINDOW + QBLK)[None, :]
    dist = qq + WINDOW - kk
    wmap = np.where((dist >= 0) & (dist < WINDOW), _rel_bucket_np(dist), -1)
    u = np.arange(LANES)[None, :]
    dist_c = qq - CMP_STRIDE * (u - CNEAR_LO) - (CMP_LEN - 1)
    cmap = np.where((dist_c >= 0) & (u < CNEAR), _rel_bucket_np(dist_c), -1)
    cmap = np.where(u < CNEAR, cmap, -2)
    return wmap.astype(np.int32), cmap.astype(np.int32)


def _tables_kernel(bias_ref, wmap_ref, cmap_ref, tw_ref, tc_ref):
    h = pl.program_id(0)
    far = bias_ref[N_BUCKETS - 1, h]

    def lookup(m):
        out = jnp.where(m == -1, NEG_INF, 0.0).astype(F32)
        for b in range(N_BUCKETS):
            out = jnp.where(m == b, (bias_ref[b, h] - far) * LOG2E, out)
        return out

    tw_ref[0] = lookup(wmap_ref[...])
    cmap = cmap_ref[...]
    val = lookup(cmap)
    hi = val.astype(BF16)
    lo = (val - hi.astype(F32)).astype(BF16)
    col = lax.broadcasted_iota(jnp.int32, cmap.shape, 1)
    hi_part = jnp.where(col < CNEAR, hi.astype(F32), 0.0)
    lo_part = pltpu.roll(jnp.where(col < CNEAR, lo.astype(F32), 0.0), CNEAR, 1)
    one = jnp.where(col == 2 * CNEAR, 1.0, 0.0)
    tc_ref[0] = (hi_part + lo_part + one).astype(BF16)


def _bias_tables(rel_bias):
    wmap, cmap = _static_maps()
    nw = WINDOW + QBLK
    return pl.pallas_call(
        _tables_kernel,
        grid=(N_HEADS,),
        in_specs=[
            pl.BlockSpec(memory_space=pltpu.SMEM),
            pl.BlockSpec((QBLK, nw), lambda h: (0, 0)),
            pl.BlockSpec((QBLK, LANES), lambda h: (0, 0)),
        ],
        out_specs=[
            pl.BlockSpec((1, QBLK, nw), lambda h: (h, 0, 0)),
            pl.BlockSpec((1, QBLK, LANES), lambda h: (h, 0, 0)),
        ],
        out_shape=[
            jax.ShapeDtypeStruct((N_HEADS, QBLK, nw), F32),
            jax.ShapeDtypeStruct((N_HEADS, QBLK, LANES), BF16),
        ],
        compiler_params=_params("parallel"),
        name="bias_tables",
    )(rel_bias, jnp.asarray(wmap), jnp.asarray(cmap))


def _nsa_tables(S):
    ncp = S // CMP_STRIDE
    nb = LANES
    pos = np.arange(S + KTILE)
    eblk = (pos[:, None] // SEL_BLOCK == np.arange(nb)[None, :]) & (pos[:, None] < S)
    cs = np.arange(ncp)[None, :] * CMP_STRIDE
    ss = np.arange(nb)[:, None] * SEL_BLOCK
    ovlt = (cs < ss + SEL_BLOCK) & (cs + CMP_LEN > ss) & (np.arange(ncp)[None, :] < ncp - 1) & (ss < S)
    rel = np.arange(2 * ncp)[:, None] - ncp + CNEAR_LO
    u = np.arange(LANES)[None, :]
    place = np.where((rel == u) & (u < CNEAR), 1.0, 0.0)
    place = np.where((rel == u - CNEAR) & (u >= CNEAR) & (u < 2 * CNEAR), 1.0, place)
    place = np.where((u == 2 * CNEAR) & (rel >= CNEAR), NEG_BF16, place)
    return jnp.asarray(eblk, BF16), jnp.asarray(ovlt, BF16), jnp.asarray(place, BF16)


def _nsa_kernel(qt_ref, kc_ref, vc_ref, eb_ref, ks_ref, vs_ref, kw_ref, vw_ref, gate_ref, tw_ref, tc_ref,
                ovl_ref, place_ref, o_ref, qaug_ref, s0_ref, s1_ref, m_ref, acc_ref, oc_next_ref, msel_next_ref):
    i = pl.program_id(2)
    ncp = kc_ref.shape[2]
    nb = ovl_ref.shape[0]
    pad_tile = ks_ref.shape[2] // KTILE - 1
    col_max = lambda a: jnp.max(a, axis=0, keepdims=True)
    prob = lambda a: jnp.exp2(a).astype(BF16)
    tile_off = lambda t, ok: pl.multiple_of(jnp.where(ok, t, pad_tile) * KTILE, KTILE)

    def query_operand(q_blk):
        row = lax.broadcasted_iota(jnp.int32, (LANES - HEAD_DIM, QROWS), 0)
        return jnp.concatenate([q_blk, jnp.where(row == 0, NEG_BF16, 0.0).astype(BF16)], axis=0)

    def select_operands(q_blk, blk):
        place = place_ref[pl.ds(pl.multiple_of(ncp - CMP_STRIDE * blk, CMP_STRIDE), ncp), :]
        return place, jnp.concatenate([query_operand(q_blk), tc_ref[0]], axis=0)

    def cmp_logits(place, rhs, c0, rows):
        return _dot(jnp.concatenate([kc_ref[0, 0, c0:c0 + rows, :], place[c0:c0 + rows]], axis=1), rhs)

    def cmp_values(c0, rows):
        return jnp.concatenate([vc_ref[0, 0, :, c0:c0 + rows], ovl_ref[:, c0:c0 + rows]], axis=0)

    def stage(state, s, vt):
        if state is None:
            m = jnp.maximum(col_max(s), M_INIT)
            return m, _dot(vt, prob(s - m))
        m_prev, acc = state
        m = jnp.maximum(m_prev, col_max(s))
        return m, jnp.exp2(m_prev - m) * acc + _dot(vt, prob(s - m))

    def select_store(acc, blk):
        inv_l = 1.0 / jnp.maximum(acc[HEAD_DIM:HEAD_DIM + 1, :], 1e-30)
        imp4 = acc[VROWS:VROWS + nb, :] * inv_l
        imp = imp4[:, 0:QBLK] + imp4[:, QBLK:2 * QBLK] + imp4[:, 2 * QBLK:3 * QBLK] + imp4[:, 3 * QBLK:4 * QBLK]
        jb = lax.broadcasted_iota(jnp.int32, (nb, QBLK), 0)
        qq = lax.broadcasted_iota(jnp.int32, (nb, QBLK), 1)
        cur = (QBLK // SEL_BLOCK) * blk + (qq >> 6)
        forced = (jb == 0) | (jb == cur) | (jb == cur - 1)
        sc0 = jnp.where(forced, TAKEN, jnp.where(jb <= cur, imp, NEG_INF))
        n_rounds = N_SELECT - N_FORCED

        def store(sc):
            msel = jnp.where(sc == TAKEN, 0.0, NEG_BF16).astype(BF16)
            msel_next_ref[...] = jnp.concatenate([msel] * Q_PER_KV, axis=1)

        sc = sc0
        for _ in range(n_rounds):
            sc = jnp.where(sc == col_max(sc), TAKEN, sc)
        store(sc)
        oc_next_ref[...] = acc[0:HEAD_DIM, :] * inv_l

        picked = jnp.where((jb <= cur) & jnp.logical_not(forced) & (sc == TAKEN), 1.0, 0.0)
        most = jnp.max(jnp.sum(picked, axis=0, keepdims=True))

        @pl.when(most > n_rounds)
        def _():
            jio = jb.astype(F32)
            sc = sc0
            for _ in range(n_rounds):
                idx = jnp.min(jnp.where(sc == col_max(sc), jio, float(nb)), axis=0, keepdims=True)
                sc = jnp.where(jio == idx, TAKEN, sc)
            store(sc)

    @pl.when(i == 0)
    def _():
        place, rhs = select_operands(qt_ref[0, 0, 0], 0)
        select_store(stage(None, cmp_logits(place, rhs, 0, ncp), cmp_values(0, ncp))[1], 0)

    qn = query_operand(qt_ref[0, 0, i])
    qaug_ref[nb:nb + LANES, :] = qn
    qaug_ref[0:nb, :] = msel_next_ref[...]
    o_c = oc_next_ref[...]

    def sel_logits(o):
        keys = jnp.concatenate([eb_ref[pl.ds(o, KTILE), :], ks_ref[0, 0, pl.ds(o, KTILE), :]], axis=1)
        return _dot(keys, qaug_ref[...])

    n_far = jnp.maximum(i - 1, 0)
    far_off = lambda t: tile_off(t, t < n_far)

    nxt = jnp.minimum(i + 1, pl.num_programs(2) - 1)
    place_n, rhs_n = select_operands(qt_ref[0, 0, nxt], nxt)
    ct = min(KTILE, ncp)
    jobs = [("cmp", ct, functools.partial(cmp_logits, place_n, rhs_n, c0, ct), functools.partial(cmp_values, c0, ct))
            for c0 in range(0, ncp, ct)]
    for tt in range(WINDOW // KTILE + 1):
        t = i - WINDOW // KTILE + tt
        o = tile_off(t, t >= 0)
        jobs.append(("win", KTILE,
                     lambda o=o, tt=tt: (_dot(kw_ref[0, 0, pl.ds(o, KTILE), :], qn)
                                         + tw_ref[0, tt * KTILE:(tt + 1) * KTILE, :]),
                     lambda o=o: vw_ref[0, 0, :, pl.ds(o, KTILE)]))
    for o, t0 in ((tile_off(i - 1, i >= 1), WINDOW - KTILE), (pl.multiple_of(i * KTILE, KTILE), WINDOW)):
        jobs.append(("sel", KTILE, lambda o=o, t0=t0: sel_logits(o) + tw_ref[0, t0:t0 + KTILE, :],
                     lambda o=o: vs_ref[0, 0, :, pl.ds(o, KTILE)]))
    jobs.append(("far", KTILE, lambda: sel_logits(far_off(0)), None))

    bufs = (s0_ref, s1_ref)
    bufs[0][0:jobs[0][1], :] = jobs[0][2]()
    state = {}
    for k, (branch, rows, _, values) in enumerate(jobs[:-1]):
        bufs[(k + 1) % 2][0:jobs[k + 1][1], :] = jobs[k + 1][2]()
        state[branch] = stage(state.get(branch), bufs[k % 2][0:rows, :], values())
    far_bufs = (bufs[(len(jobs) - 1) % 2], bufs[len(jobs) % 2])
    acc_w = state["win"][1]
    m_ref[...] = jnp.broadcast_to(state["sel"][0], m_ref.shape)
    acc_ref[...] = state["sel"][1]
    select_store(state["cmp"][1], nxt)

    def update(s_ref, t):
        o = far_off(t)
        sl = s_ref[...]
        m_prev = m_ref[...]
        m_next = jnp.maximum(m_prev, col_max(sl))
        alpha = jnp.exp2(m_prev - m_next)
        pt = prob(sl - m_next[0:1, :])
        acc_ref[...] = alpha[0:1, :] * acc_ref[...] + _dot(vs_ref[0, 0, :, pl.ds(o, KTILE)], pt)
        m_ref[...] = m_next

    def far_trips(first_tile, n_trips, unroll):
        def body(k, carry):
            for h in range(unroll):
                t = first_tile + unroll * k + h
                far_bufs[(h + 1) % 2][...] = sel_logits(far_off(t + 1))
                update(far_bufs[h % 2], t)
            return carry

        lax.fori_loop(0, n_trips, body, 0)

    done = 0
    for unroll in FAR_UNROLLS[:-1]:
        n_trips = (n_far - done) // unroll
        far_trips(done, n_trips, unroll)
        done = done + n_trips * unroll
    last = FAR_UNROLLS[-1]
    far_trips(done, (n_far - done + last - 1) // last, last)
    acc_s = acc_ref[...]

    gb = gate_ref[0, i, 0]
    gate = lambda br: jnp.concatenate(
        [gb[r * N_BRANCH + br:r * N_BRANCH + br + 1, :] for r in range(Q_PER_KV)], axis=1)
    out = (gate(0) * o_c
           + (gate(1) / acc_s[HEAD_DIM:HEAD_DIM + 1, :]) * acc_s[0:HEAD_DIM, :]
           + (gate(2) / acc_w[HEAD_DIM:HEAD_DIM + 1, :]) * acc_w[0:HEAD_DIM, :]).astype(BF16)
    q_off = pl.multiple_of(i * QBLK, QBLK)
    for r in range(Q_PER_KV):
        o_ref[0, r * HEAD_DIM:(r + 1) * HEAD_DIM, pl.ds(q_off, QBLK)] = out[:, r * QBLK:(r + 1) * QBLK]


def _nsa(qt, kc, vct, eblk, ks, vst, kw, vwt, gates, twt, tct, ovlt, place):
    B, G, NQ, _, _ = qt.shape
    S = NQ * QBLK
    nb = ovlt.shape[0]
    per_bg = lambda a: pl.BlockSpec((1, 1) + a.shape[2:], lambda b, g, i: (b, g) + (0,) * (a.ndim - 2))
    per_g = lambda a: pl.BlockSpec((1,) + a.shape[1:], lambda b, g, i: (g,) + (0,) * (a.ndim - 1))
    const = lambda a: pl.BlockSpec(a.shape, lambda b, g, i: (0,) * a.ndim)
    return pl.pallas_call(
        _nsa_kernel,
        grid=(B, G, NQ),
        in_specs=[per_bg(qt),
                  per_bg(kc), per_bg(vct), const(eblk), per_bg(ks), per_bg(vst), per_bg(kw), per_bg(vwt),
                  pl.BlockSpec((1, NQ, 1, Q_PER_KV * N_BRANCH, QBLK), lambda b, g, i: (b, 0, g, 0, 0)),
                  per_g(twt), per_g(tct), const(ovlt), const(place)],
        out_specs=pl.BlockSpec((1, Q_PER_KV * HEAD_DIM, S), lambda b, g, i: (b, g, 0)),
        out_shape=jax.ShapeDtypeStruct((B, NSA_WIDTH, S), BF16),
        scratch_shapes=[pltpu.VMEM((nb + LANES, QROWS), BF16),
                        pltpu.VMEM((KTILE, QROWS), F32), pltpu.VMEM((KTILE, QROWS), F32),
                        pltpu.VMEM((8, QROWS), F32), pltpu.VMEM((VROWS, QROWS), F32),
                        pltpu.VMEM((HEAD_DIM, QROWS), F32), pltpu.VMEM((nb, QROWS), BF16)],
        compiler_params=_params("parallel", "parallel", "arbitrary"),
        name="nsa",
    )(qt, kc, vct, eblk, ks, vst, kw, vwt, gates, twt, tct, ovlt, place)


def _mem_kv_kernel(mem_ref, g_ref, wk_ref, wv_ref, k_ref, v_ref):
    m = _rms(mem_ref[0], g_ref[...]).astype(BF16)
    k_ref[0] = _dot(m, wk_ref[...]).astype(BF16)
    v_ref[0] = _dot(m, wv_ref[...]).astype(BF16)


def _mem_kv(mem, g, wk, wv):
    B, M, _ = mem.shape
    full = lambda a: pl.BlockSpec(a.shape, lambda b: (0,) * a.ndim)
    per_b = pl.BlockSpec((1, M, D_MODEL), lambda b: (b, 0, 0))
    return pl.pallas_call(
        _mem_kv_kernel,
        grid=(B,),
        in_specs=[per_b, full(g), full(wk), full(wv)],
        out_specs=[per_b, per_b],
        out_shape=[jax.ShapeDtypeStruct((B, M, D_MODEL), BF16)] * 2,
        compiler_params=_params("parallel"),
        name="mem_kv",
    )(mem, g, wk, wv)


def _mix_xattn_kernel(x_ref, yp_ref, yc_ref, ynt_ref, wmix_ref, gmix_ref,
                      gpre_ref, wq_ref, k_ref, v_ref, wo_ref, gpost_ref, o_ref):
    ts = x_ref.shape[1]
    subs = [slice(r, r + XA_SUB) for r in range(0, ts, XA_SUB)]
    mix = [_dot(yp_ref[0, r, :], wmix_ref[0:256, :]) + _dot(yc_ref[0, r, :], wmix_ref[256:512, :])
           + lax.dot_general(ynt_ref[0, :, r], wmix_ref[512:1024, :], (((0,), (0,)), ((), ())),
                             preferred_element_type=F32) for r in subs]
    x = [x_ref[0, r, :] + _rms(m, gmix_ref[...]) for r, m in zip(subs, mix)]
    h = [_rms(xr, gpre_ref[...]).astype(BF16) for xr in x]
    qx = [(_dot(hr, wq_ref[...]) * (XA_HEAD_DIM ** -0.5)).astype(BF16) for hr in h]
    attn = []
    for qr in qx:
        outs = []
        for hh in range(XA_HEADS):
            sl = slice(hh * XA_HEAD_DIM, (hh + 1) * XA_HEAD_DIM)
            s = _dot_nt(qr[:, sl], k_ref[0, :, sl])
            p = jnp.exp(s - jnp.max(s, axis=1, keepdims=True))
            p = p * (1.0 / jnp.sum(p, axis=1, keepdims=True))
            outs.append(_dot(p.astype(BF16), v_ref[0, :, sl]).astype(BF16))
        attn.append(jnp.concatenate(outs, axis=1))
    for r, xr, o in zip(subs, x, attn):
        o_ref[0, r, :] = xr + _rms(_dot(o, wo_ref[...]), gpost_ref[...])


def _mix_xattn(x, yp, yc, ynt, wmix, gmix, gpre, wq, km, vm, wo, gpost):
    B, S, _ = x.shape
    M = km.shape[1]
    ts = XA_TILE
    full = lambda a: pl.BlockSpec(a.shape, lambda b, s: (0,) * a.ndim)
    seq = lambda n: pl.BlockSpec((1, ts, n), lambda b, s: (b, s, 0))
    per_b = pl.BlockSpec((1, M, D_MODEL), lambda b, s: (b, 0, 0))
    return pl.pallas_call(
        _mix_xattn_kernel,
        grid=(B, S // ts),
        in_specs=[seq(D_MODEL), seq(POOL_WIDTH), seq(CONV_WIDTH),
                  pl.BlockSpec((1, NSA_WIDTH, ts), lambda b, s: (b, 0, s)), full(wmix), full(gmix),
                  full(gpre), full(wq), per_b, per_b, full(wo), full(gpost)],
        out_specs=seq(D_MODEL),
        out_shape=jax.ShapeDtypeStruct((B, S, D_MODEL), F32),
        compiler_params=_params("parallel", "parallel"),
        name="mix_xattn",
    )(x, yp, yc, ynt, wmix, gmix, gpre, wq, km, vm, wo, gpost)


def _mlp_kernel(x_ref, gpre_ref, w1_ref, w2_ref, gpost_ref, o_ref):
    x = x_ref[...]
    h = _rms(x, gpre_ref[...]).astype(BF16)
    y = jnp.zeros(x.shape, F32)
    chunk = 1024
    for c in range(D_FF // chunk):
        a = jnp.maximum(_dot(h, w1_ref[:, c * chunk:(c + 1) * chunk]), 0.0)
        y = y + _dot((a * a).astype(BF16), w2_ref[c * chunk:(c + 1) * chunk, :])
    o_ref[...] = x + _rms(y, gpost_ref[...])


def _mlp(xf, gpre, w1, w2, gpost):
    T = xf.shape[0]
    row = pl.BlockSpec((ROW_TILE, D_MODEL), lambda i: (i, 0))
    full = lambda a: pl.BlockSpec(a.shape, lambda i: (0,) * a.ndim)
    return pl.pallas_call(
        _mlp_kernel,
        grid=(T // ROW_TILE,),
        in_specs=[row, full(gpre), full(w1), full(w2), full(gpost)],
        out_specs=row,
        out_shape=jax.ShapeDtypeStruct((T, D_MODEL), F32),
        compiler_params=_params("parallel"),
        name="mlp",
    )(xf, gpre, w1, w2, gpost)


def kernel(x, mem, rel_bias, mix_pre_g, mix_post_g, w_in, pool_w, pool_scale, conv_w, conv_b, conv_ln_g,
           conv_ln_b, conv_pw, cmp_k_pos, cmp_k_w1, cmp_k_w2, cmp_v_pos, cmp_v_w1, cmp_v_w2, w_out,
           xa_pre_g, xa_post_g, mem_g, xa_wq, xa_wk, xa_wv, xa_wo, mlp_pre_g, mlp_post_g, mlp_w1, mlp_w2):
    B, S, _ = x.shape
    assert S % ROW_TILE == 0 and S // SEL_BLOCK >= N_SELECT and S // SEL_BLOCK <= LANES
    T = B * S
    ncp = S // CMP_STRIDE
    row2 = lambda a: a.reshape(1, -1)

    tw, tc = _bias_tables(rel_bias)
    twt = tw.reshape(N_KV, QROWS, WINDOW + QBLK).transpose(0, 2, 1)
    tct = tc.reshape(N_KV, QROWS, LANES).transpose(0, 2, 1)
    eblk, ovlt, place = _nsa_tables(S)

    for l in range(DEPTH):
        w_a, w_bt = _in_proj_weights(w_in[l])
        up, uc, kcr, ks, kw, qt, vst, vwt, gates = _in_proj(x, row2(mix_pre_g[l]), w_a, w_bt)
        ks, kw, vst, vwt = _pad_tiles(ks, kw, vst, vwt)

        pw_bd = jnp.zeros((POOL_WIDTH, POOL_WIDTH), F32)
        for g in range(len(POOL_WINDOWS)):
            pw_bd = lax.dynamic_update_slice(pw_bd, pool_w[l, g], (g * POOL_GROUP, g * POOL_GROUP))
        yp, yc = _mixers(up, uc, pw_bd.astype(BF16), row2(pool_scale[l]), conv_w[l], row2(conv_b[l]),
                         row2(conv_ln_g[l]), row2(conv_ln_b[l]), conv_pw[l].astype(BF16))

        cpos = jnp.stack([cmp_k_pos[l], cmp_v_pos[l]]).reshape(2, 2, 1, CMP_STRIDE * HEAD_DIM)
        cw1 = jnp.stack([cmp_k_w1[l], cmp_v_w1[l]]).astype(BF16)
        kc, vct = _compress(kcr, cpos, cw1, cmp_k_w2[l].astype(BF16), cmp_v_w2[l].T.astype(BF16))

        ynt = _nsa(qt, kc, vct, eblk, ks, vst, kw, vwt, gates, twt, tct, ovlt, place)

        km, vm = _mem_kv(mem, row2(mem_g[l]), xa_wk[l].astype(BF16), xa_wv[l].astype(BF16))
        x = _mix_xattn(x, yp, yc, ynt, w_out[l].astype(BF16), row2(mix_post_g[l]), row2(xa_pre_g[l]),
                       xa_wq[l].astype(BF16), km, vm, xa_wo[l].astype(BF16), row2(xa_post_g[l]))

        x = _mlp(x.reshape(T, D_MODEL), row2(mlp_pre_g[l]), mlp_w1[l].astype(BF16), mlp_w2[l].astype(BF16),
                 row2(mlp_post_g[l])).reshape(B, S, D_MODEL)
    return x
```

```python
import functools
import math

import numpy as np
import jax
import jax.numpy as jnp
from jax import lax
from jax.experimental import pallas as pl
from jax.experimental.pallas import tpu as pltpu

F32 = jnp.float32
BF16 = jnp.bfloat16

D_MODEL = 1024
DEPTH = 2
POOL_WIDTH = 256
POOL_WINDOWS = (2, 4, 8, 16)
POOL_GROUP = 64
CONV_WIDTH = 256
CONV_KSIZE = 31
NSA_WIDTH = 512
HEAD_DIM = 64
N_HEADS = 8
N_KV = 2
Q_PER_KV = 4
CMP_LEN = 32
CMP_STRIDE = 16
CMP_HIDDEN = 256
SEL_BLOCK = 64
N_SELECT = 16
WINDOW = 512
N_BRANCH = 3
N_BUCKETS = 32
MAX_EXACT = 16
MAX_DISTANCE = 128
XA_HEADS = 4
XA_HEAD_DIM = 256
D_FF = 4096
EPS = 1e-6
NEG_INF = -1e30

LANES = 128
ROW_TILE = 512
XA_SUB = 512
XA_TILE = 1024
QBLK = 256
KTILE = 256
QROWS = Q_PER_KV * QBLK
HALO = 32
CNEAR = 24
CNEAR_LO = 8
NEG_BF16 = -(2.0 ** 100)
M_INIT = -1e29
TAKEN = -3e38
N_FORCED = 3
LOG2E = 1.4426950408889634
FAR_UNROLLS = (16, 8, 4, 2, 1)
VROWS = 80
VMEM_LIMIT = 56 * 1024 * 1024


def _rms(x, g):
    return x * lax.rsqrt(jnp.mean(x * x, axis=-1, keepdims=True) + EPS) * g


def _dot(a, b):
    return jnp.dot(a, b, preferred_element_type=F32)


def _dot_nt(a, b):
    return lax.dot_general(a, b, (((1,), (1,)), ((), ())), preferred_element_type=F32)


def _params(*sem):
    return pltpu.CompilerParams(dimension_semantics=sem, vmem_limit_bytes=VMEM_LIMIT)


def _in_proj_weights(w):
    z4 = jnp.zeros((D_MODEL, 4), w.dtype)
    kv0 = POOL_WIDTH + 2 * CONV_WIDTH + NSA_WIDTH
    w_a = jnp.concatenate([w[:, 0:768], w[:, kv0 + 256:kv0 + 384], w[:, kv0 + 512:kv0 + 640],
                           w[:, kv0:kv0 + 256]], axis=1)
    gw = w[:, kv0 + 768:]
    w_b = jnp.concatenate([w[:, 768:1280], w[:, kv0 + 384:kv0 + 512], w[:, kv0 + 640:kv0 + 768],
                           gw[:, 0:12], z4, gw[:, 12:24], z4], axis=1)
    return w_a.astype(BF16), w_b.T.astype(BF16)


def _in_proj_kernel(x_ref, g_ref, wa_ref, wbt_ref,
                    pool_ref, conv_ref, kc_ref, ks_ref, kw_ref, qt_ref, vs_ref, vw_ref, gate_ref):
    tm = x_ref.shape[1]
    h = _rms(x_ref[0], g_ref[...]).astype(BF16)
    pool_ref[0] = _dot(h, wa_ref[:, 0:256])
    conv_ref[0] = _dot(h, wa_ref[:, 256:768])
    ksw = _dot(h, wa_ref[:, 768:1024]).astype(BF16)
    kvc = _dot(h, wa_ref[:, 1024:1280])
    lane_pad = jnp.zeros((tm, LANES - HEAD_DIM), BF16)
    for g in range(N_KV):
        ks_ref[0, g] = jnp.concatenate([ksw[:, g * HEAD_DIM:(g + 1) * HEAD_DIM], lane_pad], axis=1)
        kw_ref[0, g] = jnp.concatenate([ksw[:, (N_KV + g) * HEAD_DIM:(N_KV + g + 1) * HEAD_DIM], lane_pad], axis=1)
        for kv in range(2):
            kc_ref[0, kv, g] = kvc[:, (kv * N_KV + g) * HEAD_DIM:(kv * N_KV + g + 1) * HEAD_DIM]

    pt = _dot_nt(wbt_ref[...], h)
    qs = (pt[0:NSA_WIDTH] * (HEAD_DIM ** -0.5 * LOG2E)).astype(BF16)
    ones_row = jnp.where(lax.broadcasted_iota(jnp.int32, (VROWS - HEAD_DIM, tm), 0) == 0, 1.0, 0.0).astype(BF16)
    gs = jax.nn.sigmoid(pt[NSA_WIDTH + 256:NSA_WIDTH + 288])
    for g in range(N_KV):
        for qb in range(tm // QBLK):
            cols = slice(qb * QBLK, (qb + 1) * QBLK)
            qt_ref[0, g, qb] = jnp.concatenate(
                [qs[(g * Q_PER_KV + r) * HEAD_DIM:(g * Q_PER_KV + r + 1) * HEAD_DIM, cols] for r in range(Q_PER_KV)],
                axis=1)
            gate_ref[0, qb, g] = gs[g * 16:g * 16 + Q_PER_KV * N_BRANCH, cols]
        v0 = NSA_WIDTH + g * HEAD_DIM
        vs_ref[0, g] = jnp.concatenate([pt[v0:v0 + HEAD_DIM].astype(BF16), ones_row], axis=0)
        vw_ref[0, g] = jnp.concatenate([pt[v0 + 128:v0 + 128 + HEAD_DIM].astype(BF16), ones_row], axis=0)


def _pad_tiles_kernel(ks_in, kw_in, vs_in, vw_in, ks_ref, kw_ref, vs_ref, vw_ref):
    del ks_in, kw_in, vs_in, vw_in
    lane = lax.broadcasted_iota(jnp.int32, ks_ref.shape, 3)
    pad_keys = jnp.where(lane == HEAD_DIM, 1.0, 0.0).astype(BF16)
    ks_ref[...] = pad_keys
    kw_ref[...] = pad_keys
    vs_ref[...] = jnp.zeros(vs_ref.shape, BF16)
    vw_ref[...] = jnp.zeros(vw_ref.shape, BF16)


def _pad_tiles(ks, kw, vst, vwt):
    B = ks.shape[0]
    last = (ks.shape[2] - KTILE) // KTILE
    anyspec = pl.BlockSpec(memory_space=pl.ANY)
    keys = pl.BlockSpec((1, N_KV, KTILE, LANES), lambda b: (b, 0, last, 0))
    vals = pl.BlockSpec((1, N_KV, VROWS, KTILE), lambda b: (b, 0, 0, last))
    return pl.pallas_call(
        _pad_tiles_kernel,
        grid=(B,),
        in_specs=[anyspec] * 4,
        out_specs=[keys, keys, vals, vals],
        out_shape=[jax.ShapeDtypeStruct(a.shape, a.dtype) for a in (ks, kw, vst, vwt)],
        input_output_aliases={0: 0, 1: 1, 2: 2, 3: 3},
        compiler_params=_params("parallel"),
        name="pad_tiles",
    )(ks, kw, vst, vwt)


def _in_proj(x, g, w_a, w_bt):
    B, S, _ = x.shape
    tm = ROW_TILE
    nq = tm // QBLK
    full = lambda a: pl.BlockSpec(a.shape, lambda b, s: (0,) * a.ndim)
    seq = lambda n: pl.BlockSpec((1, tm, n), lambda b, s: (b, s, 0))
    keys = pl.BlockSpec((1, N_KV, tm, LANES), lambda b, s: (b, 0, s, 0))
    vals = pl.BlockSpec((1, N_KV, VROWS, tm), lambda b, s: (b, 0, 0, s))
    return pl.pallas_call(
        _in_proj_kernel,
        grid=(B, S // tm),
        in_specs=[seq(D_MODEL), full(g), full(w_a), full(w_bt)],
        out_specs=[
            seq(POOL_WIDTH), seq(2 * CONV_WIDTH),
            pl.BlockSpec((1, 2, N_KV, tm, HEAD_DIM), lambda b, s: (b, 0, 0, s, 0)),
            keys, keys,
            pl.BlockSpec((1, N_KV, nq, HEAD_DIM, QROWS), lambda b, s: (b, 0, s, 0, 0)),
            vals, vals,
            pl.BlockSpec((1, nq, N_KV, Q_PER_KV * N_BRANCH, QBLK), lambda b, s: (b, s, 0, 0, 0)),
        ],
        out_shape=[
            jax.ShapeDtypeStruct((B, S, POOL_WIDTH), F32),
            jax.ShapeDtypeStruct((B, S, 2 * CONV_WIDTH), F32),
            jax.ShapeDtypeStruct((B, 2, N_KV, S, HEAD_DIM), F32),
            jax.ShapeDtypeStruct((B, N_KV, S + KTILE, LANES), BF16),
            jax.ShapeDtypeStruct((B, N_KV, S + KTILE, LANES), BF16),
            jax.ShapeDtypeStruct((B, N_KV, S // QBLK, HEAD_DIM, QROWS), BF16),
            jax.ShapeDtypeStruct((B, N_KV, VROWS, S + KTILE), BF16),
            jax.ShapeDtypeStruct((B, N_KV, VROWS, S + KTILE), BF16),
            jax.ShapeDtypeStruct((B, S // QBLK, N_KV, Q_PER_KV * N_BRANCH, QBLK), F32),
        ],
        compiler_params=_params("parallel", "parallel"),
        name="in_proj",
    )(x, g, w_a, w_bt)


def _mixers_kernel(up_ref, uc_ref, pw_ref, ps_ref, cw_ref, cb_ref, lg_ref, lb_ref, cpw_ref,
                   yp_ref, yc_ref, ubuf, hbuf):
    s = pl.program_id(1)
    ts = up_ref.shape[1]

    @pl.when(s == 0)
    def _():
        ubuf[0:HALO, :] = jnp.zeros((HALO, POOL_WIDTH), F32)
        hbuf[0:HALO, :] = jnp.zeros((HALO, CONV_WIDTH), F32)

    ext = HALO + ts
    back = lambda a, k: a + pltpu.roll(a, k, 0)

    u = up_ref[0]
    ubuf[HALO:ext, :] = u
    xl = ubuf[:, 0:128]
    xh = ubuf[:, 128:256]
    s2l = back(xl, 1)
    s4l = back(s2l, 2)
    s8h = back(back(back(xh, 1), 2), 4)
    s16h = back(s8h, 8)
    t1 = (s * ts + lax.broadcasted_iota(jnp.int32, (ts, 128), 0) + 1).astype(F32)
    lane = lax.broadcasted_iota(jnp.int32, (ts, 128), 1)
    first = lane < POOL_GROUP
    sum_lo = jnp.where(first, s2l[HALO:ext], s4l[HALO:ext])
    den_lo = jnp.minimum(t1, jnp.where(first, 2.0, 4.0))
    sum_hi = jnp.where(first, s8h[HALO:ext], s16h[HALO:ext])
    den_hi = jnp.minimum(t1, jnp.where(first, 8.0, 16.0))
    d = jnp.concatenate([sum_lo / den_lo - u[:, 0:128], sum_hi / den_hi - u[:, 128:256]], axis=1)
    yp_ref[0] = (_dot(d.astype(BF16), pw_ref[...]) * ps_ref[...]).astype(BF16)
    ubuf[0:HALO, :] = ubuf[ts:ext, :]

    uc = uc_ref[0]
    hbuf[HALO:ext, :] = uc[:, 0:CONV_WIDTH] * jax.nn.sigmoid(uc[:, CONV_WIDTH:])
    halves = []
    for c0 in range(0, CONV_WIDTH, LANES):
        hx = hbuf[:, c0:c0 + LANES]
        ahead = [hx] + [pltpu.roll(hx, ext - r, 0) for r in range(1, 8)]
        acc_h = jnp.zeros((ts, LANES), F32) + cb_ref[:, c0:c0 + LANES]
        for j in range(CONV_KSIZE):
            a8, r = divmod(HALO - (CONV_KSIZE - 1) + j, 8)
            acc_h = acc_h + ahead[r][8 * a8:8 * a8 + ts] * cw_ref[j:j + 1, c0:c0 + LANES]
        halves.append(acc_h)
    acc = jnp.concatenate(halves, axis=1)
    mu = jnp.mean(acc, axis=-1, keepdims=True)
    var = jnp.mean(jnp.square(acc - mu), axis=-1, keepdims=True)
    y = (acc - mu) * lax.rsqrt(var + EPS) * lg_ref[...] + lb_ref[...]
    y = y * jax.nn.sigmoid(y)
    yc_ref[0] = _dot(y.astype(BF16), cpw_ref[...]).astype(BF16)
    hbuf[0:HALO, :] = hbuf[ts:ts + HALO, :]


def _mixers(up, uc, pw, ps, cw, cb, lg, lb, cpw):
    B, S, _ = up.shape
    ts = ROW_TILE
    seq = lambda n: pl.BlockSpec((1, ts, n), lambda b, s: (b, s, 0))
    full = lambda a: pl.BlockSpec(a.shape, lambda b, s: (0,) * a.ndim)
    return pl.pallas_call(
        _mixers_kernel,
        grid=(B, S // ts),
        in_specs=[seq(256), seq(512)] + [full(a) for a in (pw, ps, cw, cb, lg, lb, cpw)],
        out_specs=[seq(256), seq(256)],
        out_shape=[jax.ShapeDtypeStruct((B, S, 256), BF16)] * 2,
        scratch_shapes=[pltpu.VMEM((HALO + ts, POOL_WIDTH), F32), pltpu.VMEM((HALO + ts, CONV_WIDTH), F32)],
        compiler_params=_params("parallel", "arbitrary"),
        name="mixers",
    )(up, uc, pw, ps, cw, cb, lg, lb, cpw)


def _compress_kernel(r_ref, pos_ref, w1_ref, w2k_ref, w2vt_ref, kc_ref, vct_ref):
    ncp = r_ref.shape[3] // CMP_STRIDE
    half = CMP_STRIDE * HEAD_DIM

    def hidden(kv):
        r = jnp.concatenate([r_ref[0, kv, 0, pl.ds(l, ncp, stride=CMP_STRIDE), :] for l in range(CMP_STRIDE)],
                            axis=1)
        a = _dot((r + pos_ref[kv, 0]).astype(BF16), w1_ref[kv, 0:half, :])
        b = _dot((r + pos_ref[kv, 1]).astype(BF16), w1_ref[kv, half:2 * half, :])
        return jax.nn.gelu(a + pltpu.roll(b, ncp - 1, 0)).astype(BF16)

    kc = _dot(hidden(0), w2k_ref[...]).astype(BF16)
    kc_ref[0, 0] = jnp.concatenate([kc, jnp.zeros((ncp, LANES - HEAD_DIM), BF16)], axis=1)
    ones_row = jnp.where(lax.broadcasted_iota(jnp.int32, (VROWS - HEAD_DIM, ncp), 0) == 0, 1.0, 0.0).astype(BF16)
    vct_ref[0, 0] = jnp.concatenate([_dot_nt(w2vt_ref[...], hidden(1)).astype(BF16), ones_row], axis=0)


def _compress(r, pos, w1, w2k, w2vt):
    B, _, G, S, _ = r.shape
    ncp = S // CMP_STRIDE
    full = lambda a: pl.BlockSpec(a.shape, lambda b, g: (0,) * a.ndim)
    return pl.pallas_call(
        _compress_kernel,
        grid=(B, G),
        in_specs=[pl.BlockSpec((1, 2, 1, S, HEAD_DIM), lambda b, g: (b, 0, g, 0, 0)),
                  full(pos), full(w1), full(w2k), full(w2vt)],
        out_specs=[pl.BlockSpec((1, 1, ncp, LANES), lambda b, g: (b, g, 0, 0)),
                   pl.BlockSpec((1, 1, VROWS, ncp), lambda b, g: (b, g, 0, 0))],
        out_shape=[jax.ShapeDtypeStruct((B, G, ncp, LANES), BF16),
                   jax.ShapeDtypeStruct((B, G, VROWS, ncp), BF16)],
        compiler_params=_params("parallel", "parallel"),
        name="compress",
    )(r, pos, w1, w2k, w2vt)


def _rel_bucket_np(dist):
    n = np.maximum(dist, 0)
    nf = np.maximum(n, 1).astype(np.float64)
    large = MAX_EXACT + (np.log(nf / MAX_EXACT) / math.log(MAX_DISTANCE / MAX_EXACT)
                         * (N_BUCKETS - MAX_EXACT)).astype(np.int64)
    return np.where(n < MAX_EXACT, n, np.minimum(large, N_BUCKETS - 1)).astype(np.int32)


def _static_maps():
    qq = np.arange(QBLK)[:, None]
    kk = np.arange(WINDOW + QBLK)[None, :]
    dist = qq + WINDOW - kk
    wmap = np.where((dist >= 0) & (dist < WINDOW), _rel_bucket_np(dist), -1)
    u = np.arange(LANES)[None, :]
    dist_c = qq - CMP_STRIDE * (u - CNEAR_LO) - (CMP_LEN - 1)
    cmap = np.where((dist_c >= 0) & (u < CNEAR), _rel_bucket_np(dist_c), -1)
    cmap = np.where(u < CNEAR, cmap, -2)
    return wmap.astype(np.int32), cmap.astype(np.int32)


def _tables_kernel(bias_ref, wmap_ref, cmap_ref, tw_ref, tc_ref):
    h = pl.program_id(0)
    far = bias_ref[N_BUCKETS - 1, h]

    def lookup(m):
        out = jnp.where(m == -1, NEG_INF, 0.0).astype(F32)
        for b in range(N_BUCKETS):
            out = jnp.where(m == b, (bias_ref[b, h] - far) * LOG2E, out)
        return out

    tw_ref[0] = lookup(wmap_ref[...])
    cmap = cmap_ref[...]
    val = lookup(cmap)
    hi = val.astype(BF16)
    lo = (val - hi.astype(F32)).astype(BF16)
    col = lax.broadcasted_iota(jnp.int32, cmap.shape, 1)
    hi_part = jnp.where(col < CNEAR, hi.astype(F32), 0.0)
    lo_part = pltpu.roll(jnp.where(col < CNEAR, lo.astype(F32), 0.0), CNEAR, 1)
    one = jnp.where(col == 2 * CNEAR, 1.0, 0.0)
    tc_ref[0] = (hi_part + lo_part + one).astype(BF16)


def _bias_tables(rel_bias):
    wmap, cmap = _static_maps()
    nw = WINDOW + QBLK
    return pl.pallas_call(
        _tables_kernel,
        grid=(N_HEADS,),
        in_specs=[
            pl.BlockSpec(memory_space=pltpu.SMEM),
            pl.BlockSpec((QBLK, nw), lambda h: (0, 0)),
            pl.BlockSpec((QBLK, LANES), lambda h: (0, 0)),
        ],
        out_specs=[
            pl.BlockSpec((1, QBLK, nw), lambda h: (h, 0, 0)),
            pl.BlockSpec((1, QBLK, LANES), lambda h: (h, 0, 0)),
        ],
        out_shape=[
            jax.ShapeDtypeStruct((N_HEADS, QBLK, nw), F32),
            jax.ShapeDtypeStruct((N_HEADS, QBLK, LANES), BF16),
        ],
        compiler_params=_params("parallel"),
        name="bias_tables",
    )(rel_bias, jnp.asarray(wmap), jnp.asarray(cmap))


def _nsa_tables(S):
    ncp = S // CMP_STRIDE
    nb = LANES
    pos = np.arange(S + KTILE)
    eblk = (pos[:, None] // SEL_BLOCK == np.arange(nb)[None, :]) & (pos[:, None] < S)
    cs = np.arange(ncp)[None, :] * CMP_STRIDE
    ss = np.arange(nb)[:, None] * SEL_BLOCK
    ovlt = (cs < ss + SEL_BLOCK) & (cs + CMP_LEN > ss) & (np.arange(ncp)[None, :] < ncp - 1) & (ss < S)
    rel = np.arange(2 * ncp)[:, None] - ncp + CNEAR_LO
    u = np.arange(LANES)[None, :]
    place = np.where((rel == u) & (u < CNEAR), 1.0, 0.0)
    place = np.where((rel == u - CNEAR) & (u >= CNEAR) & (u < 2 * CNEAR), 1.0, place)
    place = np.where((u == 2 * CNEAR) & (rel >= CNEAR), NEG_BF16, place)
    return jnp.asarray(eblk, BF16), jnp.asarray(ovlt, BF16), jnp.asarray(place, BF16)


def _nsa_kernel(qt_ref, kc_ref, vc_ref, eb_ref, ks_ref, vs_ref, kw_ref, vw_ref, gate_ref, tw_ref, tc_ref,
                ovl_ref, place_ref, o_ref, qaug_ref, s0_ref, s1_ref, m_ref, acc_ref, oc_next_ref, msel_next_ref):
    i = pl.program_id(2)
    ncp = kc_ref.shape[2]
    nb = ovl_ref.shape[0]
    pad_tile = ks_ref.shape[2] // KTILE - 1
    col_max = lambda a: jnp.max(a, axis=0, keepdims=True)
    prob = lambda a: jnp.exp2(a).astype(BF16)
    tile_off = lambda t, ok: pl.multiple_of(jnp.where(ok, t, pad_tile) * KTILE, KTILE)

    def query_operand(q_blk):
        row = lax.broadcasted_iota(jnp.int32, (LANES - HEAD_DIM, QROWS), 0)
        return jnp.concatenate([q_blk, jnp.where(row == 0, NEG_BF16, 0.0).astype(BF16)], axis=0)

    def select_operands(q_blk, blk):
        place = place_ref[pl.ds(pl.multiple_of(ncp - CMP_STRIDE * blk, CMP_STRIDE), ncp), :]
        return place, jnp.concatenate([query_operand(q_blk), tc_ref[0]], axis=0)

    def cmp_logits(place, rhs, c0, rows):
        return _dot(jnp.concatenate([kc_ref[0, 0, c0:c0 + rows, :], place[c0:c0 + rows]], axis=1), rhs)

    def cmp_values(c0, rows):
        return jnp.concatenate([vc_ref[0, 0, :, c0:c0 + rows], ovl_ref[:, c0:c0 + rows]], axis=0)

    def stage(state, s, vt):
        if state is None:
            m = jnp.maximum(col_max(s), M_INIT)
            return m, _dot(vt, prob(s - m))
        m_prev, acc = state
        m = jnp.maximum(m_prev, col_max(s))
        return m, jnp.exp2(m_prev - m) * acc + _dot(vt, prob(s - m))

    def select_store(acc, blk):
        inv_l = 1.0 / jnp.maximum(acc[HEAD_DIM:HEAD_DIM + 1, :], 1e-30)
        imp4 = acc[VROWS:VROWS + nb, :] * inv_l
        imp = imp4[:, 0:QBLK] + imp4[:, QBLK:2 * QBLK] + imp4[:, 2 * QBLK:3 * QBLK] + imp4[:, 3 * QBLK:4 * QBLK]
        jb = lax.broadcasted_iota(jnp.int32, (nb, QBLK), 0)
        qq = lax.broadcasted_iota(jnp.int32, (nb, QBLK), 1)
        cur = (QBLK // SEL_BLOCK) * blk + (qq >> 6)
        forced = (jb == 0) | (jb == cur) | (jb == cur - 1)
        sc0 = jnp.where(forced, TAKEN, jnp.where(jb <= cur, imp, NEG_INF))
        n_rounds = N_SELECT - N_FORCED

        def store(sc):
            msel = jnp.where(sc == TAKEN, 0.0, NEG_BF16).astype(BF16)
            msel_next_ref[...] = jnp.concatenate([msel] * Q_PER_KV, axis=1)

        sc = sc0
        for _ in range(n_rounds):
            sc = jnp.where(sc == col_max(sc), TAKEN, sc)
        store(sc)
        oc_next_ref[...] = acc[0:HEAD_DIM, :] * inv_l

        picked = jnp.where((jb <= cur) & jnp.logical_not(forced) & (sc == TAKEN), 1.0, 0.0)
        most = jnp.max(jnp.sum(picked, axis=0, keepdims=True))

        @pl.when(most > n_rounds)
        def _():
            jio = jb.astype(F32)
            sc = sc0
            for _ in range(n_rounds):
                idx = jnp.min(jnp.where(sc == col_max(sc), jio, float(nb)), axis=0, keepdims=True)
                sc = jnp.where(jio == idx, TAKEN, sc)
            store(sc)

    @pl.when(i == 0)
    def _():
        place, rhs = select_operands(qt_ref[0, 0, 0], 0)
        select_store(stage(None, cmp_logits(place, rhs, 0, ncp), cmp_values(0, ncp))[1], 0)

    qn = query_operand(qt_ref[0, 0, i])
    qaug_ref[nb:nb + LANES, :] = qn
    qaug_ref[0:nb, :] = msel_next_ref[...]
    o_c = oc_next_ref[...]

    def sel_logits(o):
        keys = jnp.concatenate([eb_ref[pl.ds(o, KTILE), :], ks_ref[0, 0, pl.ds(o, KTILE), :]], axis=1)
        return _dot(keys, qaug_ref[...])

    n_far = jnp.maximum(i - 1, 0)
    far_off = lambda t: tile_off(t, t < n_far)

    nxt = jnp.minimum(i + 1, pl.num_programs(2) - 1)
    place_n, rhs_n = select_operands(qt_ref[0, 0, nxt], nxt)
    ct = min(KTILE, ncp)
    jobs = [("cmp", ct, functools.partial(cmp_logits, place_n, rhs_n, c0, ct), functools.partial(cmp_values, c0, ct))
            for c0 in range(0, ncp, ct)]
    for tt in range(WINDOW // KTILE + 1):
        t = i - WINDOW // KTILE + tt
        o = tile_off(t, t >= 0)
        jobs.append(("win", KTILE,
                     lambda o=o, tt=tt: (_dot(kw_ref[0, 0, pl.ds(o, KTILE), :], qn)
                                         + tw_ref[0, tt * KTILE:(tt + 1) * KTILE, :]),
                     lambda o=o: vw_ref[0, 0, :, pl.ds(o, KTILE)]))
    for o, t0 in ((tile_off(i - 1, i >= 1), WINDOW - KTILE), (pl.multiple_of(i * KTILE, KTILE), WINDOW)):
        jobs.append(("sel", KTILE, lambda o=o, t0=t0: sel_logits(o) + tw_ref[0, t0:t0 + KTILE, :],
                     lambda o=o: vs_ref[0, 0, :, pl.ds(o, KTILE)]))
    jobs.append(("far", KTILE, lambda: sel_logits(far_off(0)), None))

    bufs = (s0_ref, s1_ref)
    bufs[0][0:jobs[0][1], :] = jobs[0][2]()
    state = {}
    for k, (branch, rows, _, values) in enumerate(jobs[:-1]):
        bufs[(k + 1) % 2][0:jobs[k + 1][1], :] = jobs[k + 1][2]()
        state[branch] = stage(state.get(branch), bufs[k % 2][0:rows, :], values())
    far_bufs = (bufs[(len(jobs) - 1) % 2], bufs[len(jobs) % 2])
    acc_w = state["win"][1]
    m_ref[...] = jnp.broadcast_to(state["sel"][0], m_ref.shape)
    acc_ref[...] = state["sel"][1]
    select_store(state["cmp"][1], nxt)

    def update(s_ref, t):
        o = far_off(t)
        sl = s_ref[...]
        m_prev = m_ref[...]
        m_next = jnp.maximum(m_prev, col_max(sl))
        alpha = jnp.exp2(m_prev - m_next)
        pt = prob(sl - m_next[0:1, :])
        acc_ref[...] = alpha[0:1, :] * acc_ref[...] + _dot(vs_ref[0, 0, :, pl.ds(o, KTILE)], pt)
        m_ref[...] = m_next

    def far_trips(first_tile, n_trips, unroll):
        def body(k, carry):
            for h in range(unroll):
                t = first_tile + unroll * k + h
                far_bufs[(h + 1) % 2][...] = sel_logits(far_off(t + 1))
                update(far_bufs[h % 2], t)
            return carry

        lax.fori_loop(0, n_trips, body, 0)

    done = 0
    for unroll in FAR_UNROLLS[:-1]:
        n_trips = (n_far - done) // unroll
        far_trips(done, n_trips, unroll)
        done = done + n_trips * unroll
    last = FAR_UNROLLS[-1]
    far_trips(done, (n_far - done + last - 1) // last, last)
    acc_s = acc_ref[...]

    gb = gate_ref[0, i, 0]
    gate = lambda br: jnp.concatenate(
        [gb[r * N_BRANCH + br:r * N_BRANCH + br + 1, :] for r in range(Q_PER_KV)], axis=1)
    out = (gate(0) * o_c
           + (gate(1) / acc_s[HEAD_DIM:HEAD_DIM + 1, :]) * acc_s[0:HEAD_DIM, :]
           + (gate(2) / acc_w[HEAD_DIM:HEAD_DIM + 1, :]) * acc_w[0:HEAD_DIM, :]).astype(BF16)
    q_off = pl.multiple_of(i * QBLK, QBLK)
    for r in range(Q_PER_KV):
        o_ref[0, r * HEAD_DIM:(r + 1) * HEAD_DIM, pl.ds(q_off, QBLK)] = out[:, r * QBLK:(r + 1) * QBLK]


def _nsa(qt, kc, vct, eblk, ks, vst, kw, vwt, gates, twt, tct, ovlt, place):
    B, G, NQ, _, _ = qt.shape
    S = NQ * QBLK
    nb = ovlt.shape[0]
    per_bg = lambda a: pl.BlockSpec((1, 1) + a.shape[2:], lambda b, g, i: (b, g) + (0,) * (a.ndim - 2))
    per_g = lambda a: pl.BlockSpec((1,) + a.shape[1:], lambda b, g, i: (g,) + (0,) * (a.ndim - 1))
    const = lambda a: pl.BlockSpec(a.shape, lambda b, g, i: (0,) * a.ndim)
    return pl.pallas_call(
        _nsa_kernel,
        grid=(B, G, NQ),
        in_specs=[per_bg(qt),
                  per_bg(kc), per_bg(vct), const(eblk), per_bg(ks), per_bg(vst), per_bg(kw), per_bg(vwt),
                  pl.BlockSpec((1, NQ, 1, Q_PER_KV * N_BRANCH, QBLK), lambda b, g, i: (b, 0, g, 0, 0)),
                  per_g(twt), per_g(tct), const(ovlt), const(place)],
        out_specs=pl.BlockSpec((1, Q_PER_KV * HEAD_DIM, S), lambda b, g, i: (b, g, 0)),
        out_shape=jax.ShapeDtypeStruct((B, NSA_WIDTH, S), BF16),
        scratch_shapes=[pltpu.VMEM((nb + LANES, QROWS), BF16),
                        pltpu.VMEM((KTILE, QROWS), F32), pltpu.VMEM((KTILE, QROWS), F32),
                        pltpu.VMEM((8, QROWS), F32), pltpu.VMEM((VROWS, QROWS), F32),
                        pltpu.VMEM((HEAD_DIM, QROWS), F32), pltpu.VMEM((nb, QROWS), BF16)],
        compiler_params=_params("parallel", "parallel", "arbitrary"),
        name="nsa",
    )(qt, kc, vct, eblk, ks, vst, kw, vwt, gates, twt, tct, ovlt, place)


def _mem_kv_kernel(mem_ref, g_ref, wk_ref, wv_ref, k_ref, v_ref):
    m = _rms(mem_ref[0], g_ref[...]).astype(BF16)
    k_ref[0] = _dot(m, wk_ref[...]).astype(BF16)
    v_ref[0] = _dot(m, wv_ref[...]).astype(BF16)


def _mem_kv(mem, g, wk, wv):
    B, M, _ = mem.shape
    full = lambda a: pl.BlockSpec(a.shape, lambda b: (0,) * a.ndim)
    per_b = pl.BlockSpec((1, M, D_MODEL), lambda b: (b, 0, 0))
    return pl.pallas_call(
        _mem_kv_kernel,
        grid=(B,),
        in_specs=[per_b, full(g), full(wk), full(wv)],
        out_specs=[per_b, per_b],
        out_shape=[jax.ShapeDtypeStruct((B, M, D_MODEL), BF16)] * 2,
        compiler_params=_params("parallel"),
        name="mem_kv",
    )(mem, g, wk, wv)


def _mix_xattn_kernel(x_ref, yp_ref, yc_ref, ynt_ref, wmix_ref, gmix_ref,
                      gpre_ref, wq_ref, k_ref, v_ref, wo_ref, gpost_ref, o_ref):
    ts = x_ref.shape[1]
    subs = [slice(r, r + XA_SUB) for r in range(0, ts, XA_SUB)]
    mix = [_dot(yp_ref[0, r, :], wmix_ref[0:256, :]) + _dot(yc_ref[0, r, :], wmix_ref[256:512, :])
           + lax.dot_general(ynt_ref[0, :, r], wmix_ref[512:1024, :], (((0,), (0,)), ((), ())),
                             preferred_element_type=F32) for r in subs]
    x = [x_ref[0, r, :] + _rms(m, gmix_ref[...]) for r, m in zip(subs, mix)]
    h = [_rms(xr, gpre_ref[...]).astype(BF16) for xr in x]
    qx = [(_dot(hr, wq_ref[...]) * (XA_HEAD_DIM ** -0.5)).astype(BF16) for hr in h]
    attn = []
    for qr in qx:
        outs = []
        for hh in range(XA_HEADS):
            sl = slice(hh * XA_HEAD_DIM, (hh + 1) * XA_HEAD_DIM)
            s = _dot_nt(qr[:, sl], k_ref[0, :, sl])
            p = jnp.exp(s - jnp.max(s, axis=1, keepdims=True))
            p = p * (1.0 / jnp.sum(p, axis=1, keepdims=True))
            outs.append(_dot(p.astype(BF16), v_ref[0, :, sl]).astype(BF16))
        attn.append(jnp.concatenate(outs, axis=1))
    for r, xr, o in zip(subs, x, attn):
        o_ref[0, r, :] = xr + _rms(_dot(o, wo_ref[...]), gpost_ref[...])


def _mix_xattn(x, yp, yc, ynt, wmix, gmix, gpre, wq, km, vm, wo, gpost):
    B, S, _ = x.shape
    M = km.shape[1]
    ts = XA_TILE
    full = lambda a: pl.BlockSpec(a.shape, lambda b, s: (0,) * a.ndim)
    seq = lambda n: pl.BlockSpec((1, ts, n), lambda b, s: (b, s, 0))
    per_b = pl.BlockSpec((1, M, D_MODEL), lambda b, s: (b, 0, 0))
    return pl.pallas_call(
        _mix_xattn_kernel,
        grid=(B, S // ts),
        in_specs=[seq(D_MODEL), seq(POOL_WIDTH), seq(CONV_WIDTH),
                  pl.BlockSpec((1, NSA_WIDTH, ts), lambda b, s: (b, 0, s)), full(wmix), full(gmix),
                  full(gpre), full(wq), per_b, per_b, full(wo), full(gpost)],
        out_specs=seq(D_MODEL),
        out_shape=jax.ShapeDtypeStruct((B, S, D_MODEL), F32),
        compiler_params=_params("parallel", "parallel"),
        name="mix_xattn",
    )(x, yp, yc, ynt, wmix, gmix, gpre, wq, km, vm, wo, gpost)


def _mlp_kernel(x_ref, gpre_ref, w1_ref, w2_ref, gpost_ref, o_ref):
    x = x_ref[...]
    h = _rms(x, gpre_ref[...]).astype(BF16)
    y = jnp.zeros(x.shape, F32)
    chunk = 1024
    for c in range(D_FF // chunk):
        a = jnp.maximum(_dot(h, w1_ref[:, c * chunk:(c + 1) * chunk]), 0.0)
        y = y + _dot((a * a).astype(BF16), w2_ref[c * chunk:(c + 1) * chunk, :])
    o_ref[...] = x + _rms(y, gpost_ref[...])


def _mlp(xf, gpre, w1, w2, gpost):
    T = xf.shape[0]
    row = pl.BlockSpec((ROW_TILE, D_MODEL), lambda i: (i, 0))
    full = lambda a: pl.BlockSpec(a.shape, lambda i: (0,) * a.ndim)
    return pl.pallas_call(
        _mlp_kernel,
        grid=(T // ROW_TILE,),
        in_specs=[row, full(gpre), full(w1), full(w2), full(gpost)],
        out_specs=row,
        out_shape=jax.ShapeDtypeStruct((T, D_MODEL), F32),
        compiler_params=_params("parallel"),
        name="mlp",
    )(xf, gpre, w1, w2, gpost)


def kernel(x, mem, rel_bias, mix_pre_g, mix_post_g, w_in, pool_w, pool_scale, conv_w, conv_b, conv_ln_g,
           conv_ln_b, conv_pw, cmp_k_pos, cmp_k_w1, cmp_k_w2, cmp_v_pos, cmp_v_w1, cmp_v_w2, w_out,
           xa_pre_g, xa_post_g, mem_g, xa_wq, xa_wk, xa_wv, xa_wo, mlp_pre_g, mlp_post_g, mlp_w1, mlp_w2):
    B, S, _ = x.shape
    assert S % XA_TILE == 0 and S // SEL_BLOCK >= N_SELECT and S // SEL_BLOCK <= LANES
    T = B * S
    row2 = lambda a: a.reshape(1, -1)

    tw, tc = _bias_tables(rel_bias)
    twt = tw.reshape(N_KV, QROWS, WINDOW + QBLK).transpose(0, 2, 1)
    tct = tc.reshape(N_KV, QROWS, LANES).transpose(0, 2, 1)
    eblk, ovlt, place = _nsa_tables(S)

    for l in range(DEPTH):
        w_a, w_bt = _in_proj_weights(w_in[l])
        up, uc, kcr, ks, kw, qt, vst, vwt, gates = _in_proj(x, row2(mix_pre_g[l]), w_a, w_bt)
        ks, kw, vst, vwt = _pad_tiles(ks, kw, vst, vwt)

        pw_bd = jnp.zeros((POOL_WIDTH, POOL_WIDTH), F32)
        for g in range(len(POOL_WINDOWS)):
            pw_bd = lax.dynamic_update_slice(pw_bd, pool_w[l, g], (g * POOL_GROUP, g * POOL_GROUP))
        yp, yc = _mixers(up, uc, pw_bd.astype(BF16), row2(pool_scale[l]), conv_w[l], row2(conv_b[l]),
                         row2(conv_ln_g[l]), row2(conv_ln_b[l]), conv_pw[l].astype(BF16))

        cpos = jnp.stack([cmp_k_pos[l], cmp_v_pos[l]]).reshape(2, 2, 1, CMP_STRIDE * HEAD_DIM)
        cw1 = jnp.stack([cmp_k_w1[l], cmp_v_w1[l]]).astype(BF16)
        kc, vct = _compress(kcr, cpos, cw1, cmp_k_w2[l].astype(BF16), cmp_v_w2[l].T.astype(BF16))

        ynt = _nsa(qt, kc, vct, eblk, ks, vst, kw, vwt, gates, twt, tct, ovlt, place)

        km, vm = _mem_kv(mem, row2(mem_g[l]), xa_wk[l].astype(BF16), xa_wv[l].astype(BF16))
        x = _mix_xattn(x, yp, yc, ynt, w_out[l].astype(BF16), row2(mix_post_g[l]), row2(xa_pre_g[l]),
                       xa_wq[l].astype(BF16), km, vm, xa_wo[l].astype(BF16), row2(xa_post_g[l]))

        x = _mlp(x.reshape(T, D_MODEL), row2(mlp_pre_g[l]), mlp_w1[l].astype(BF16), mlp_w2[l].astype(BF16),
                 row2(mlp_post_g[l])).reshape(B, S, D_MODEL)
    return x
```

```python
import functools
import math

import numpy as np
import jax
import jax.numpy as jnp
from jax import lax
from jax.experimental import pallas as pl
from jax.experimental.pallas import tpu as pltpu

F32 = jnp.float32
BF16 = jnp.bfloat16

D_MODEL = 1024
DEPTH = 2
POOL_WIDTH = 256
POOL_WINDOWS = (2, 4, 8, 16)
POOL_GROUP = 64
CONV_WIDTH = 256
CONV_KSIZE = 31
NSA_WIDTH = 512
HEAD_DIM = 64
N_HEADS = 8
N_KV = 2
Q_PER_KV = 4
CMP_LEN = 32
CMP_STRIDE = 16
CMP_HIDDEN = 256
SEL_BLOCK = 64
N_SELECT = 16
WINDOW = 512
N_BRANCH = 3
N_BUCKETS = 32
MAX_EXACT = 16
MAX_DISTANCE = 128
XA_HEADS = 4
XA_HEAD_DIM = 256
D_FF = 4096
EPS = 1e-6
NEG_INF = -1e30

LANES = 128
ROW_TILE = 512
XA_SUB = 512
XA_TILE = 1024
QBLK = 256
KTILE = 256
QROWS = Q_PER_KV * QBLK
HALO = 32
CNEAR = 24
CNEAR_LO = 8
NEG_BF16 = -(2.0 ** 100)
M_INIT = -1e29
TAKEN = -3e38
N_FORCED = 3
LOG2E = 1.4426950408889634
FAR_UNROLLS = (16, 8, 4, 2, 1)
VROWS = 80
VMEM_LIMIT = 56 * 1024 * 1024


def _rms(x, g):
    return x * lax.rsqrt(jnp.mean(x * x, axis=-1, keepdims=True) + EPS) * g


def _dot(a, b):
    return jnp.dot(a, b, preferred_element_type=F32)


def _dot_nt(a, b):
    return lax.dot_general(a, b, (((1,), (1,)), ((), ())), preferred_element_type=F32)


def _params(*sem):
    return pltpu.CompilerParams(dimension_semantics=sem, vmem_limit_bytes=VMEM_LIMIT)


def _in_proj_weights(w):
    z4 = jnp.zeros((D_MODEL, 4), w.dtype)
    kv0 = POOL_WIDTH + 2 * CONV_WIDTH + NSA_WIDTH
    w_a = jnp.concatenate([w[:, 0:768], w[:, kv0 + 256:kv0 + 384], w[:, kv0 + 512:kv0 + 640],
                           w[:, kv0:kv0 + 256]], axis=1)
    gw = w[:, kv0 + 768:]
    w_b = jnp.concatenate([w[:, 768:1280], w[:, kv0 + 384:kv0 + 512], w[:, kv0 + 640:kv0 + 768],
                           gw[:, 0:12], z4, gw[:, 12:24], z4], axis=1)
    return w_a.astype(BF16), w_b.T.astype(BF16)


def _in_proj_kernel(x_ref, g_ref, wa_ref, wbt_ref,
                    pool_ref, conv_ref, kc_ref, ks_ref, kw_ref, qt_ref, vs_ref, vw_ref, gate_ref):
    tm = x_ref.shape[1]
    h = _rms(x_ref[0], g_ref[...]).astype(BF16)
    pool_ref[0] = _dot(h, wa_ref[:, 0:256])
    conv_ref[0] = _dot(h, wa_ref[:, 256:768])
    ksw = _dot(h, wa_ref[:, 768:1024]).astype(BF16)
    kvc = _dot(h, wa_ref[:, 1024:1280])
    lane_pad = jnp.zeros((tm, LANES - HEAD_DIM), BF16)
    for g in range(N_KV):
        ks_ref[0, g] = jnp.concatenate([ksw[:, g * HEAD_DIM:(g + 1) * HEAD_DIM], lane_pad], axis=1)
        kw_ref[0, g] = jnp.concatenate([ksw[:, (N_KV + g) * HEAD_DIM:(N_KV + g + 1) * HEAD_DIM], lane_pad], axis=1)
        for kv in range(2):
            kc_ref[0, kv, g] = kvc[:, (kv * N_KV + g) * HEAD_DIM:(kv * N_KV + g + 1) * HEAD_DIM]

    pt = _dot_nt(wbt_ref[...], h)
    qs = (pt[0:NSA_WIDTH] * (HEAD_DIM ** -0.5 * LOG2E)).astype(BF16)
    ones_row = jnp.where(lax.broadcasted_iota(jnp.int32, (VROWS - HEAD_DIM, tm), 0) == 0, 1.0, 0.0).astype(BF16)
    gs = jax.nn.sigmoid(pt[NSA_WIDTH + 256:NSA_WIDTH + 288])
    for g in range(N_KV):
        for qb in range(tm // QBLK):
            cols = slice(qb * QBLK, (qb + 1) * QBLK)
            qt_ref[0, g, qb] = jnp.concatenate(
                [qs[(g * Q_PER_KV + r) * HEAD_DIM:(g * Q_PER_KV + r + 1) * HEAD_DIM, cols] for r in range(Q_PER_KV)],
                axis=1)
            gate_ref[0, qb, g] = gs[g * 16:g * 16 + Q_PER_KV * N_BRANCH, cols]
        v0 = NSA_WIDTH + g * HEAD_DIM
        vs_ref[0, g] = jnp.concatenate([pt[v0:v0 + HEAD_DIM].astype(BF16), ones_row], axis=0)
        vw_ref[0, g] = jnp.concatenate([pt[v0 + 128:v0 + 128 + HEAD_DIM].astype(BF16), ones_row], axis=0)


def _pad_tiles_kernel(ks_in, kw_in, vs_in, vw_in, ks_ref, kw_ref, vs_ref, vw_ref):
    del ks_in, kw_in, vs_in, vw_in
    lane = lax.broadcasted_iota(jnp.int32, ks_ref.shape, 3)
    pad_keys = jnp.where(lane == HEAD_DIM, 1.0, 0.0).astype(BF16)
    ks_ref[...] = pad_keys
    kw_ref[...] = pad_keys
    vs_ref[...] = jnp.zeros(vs_ref.shape, BF16)
    vw_ref[...] = jnp.zeros(vw_ref.shape, BF16)


def _pad_tiles(ks, kw, vst, vwt):
    B = ks.shape[0]
    last = (ks.shape[2] - KTILE) // KTILE
    anyspec = pl.BlockSpec(memory_space=pl.ANY)
    keys = pl.BlockSpec((1, N_KV, KTILE, LANES), lambda b: (b, 0, last, 0))
    vals = pl.BlockSpec((1, N_KV, VROWS, KTILE), lambda b: (b, 0, 0, last))
    return pl.pallas_call(
        _pad_tiles_kernel,
        grid=(B,),
        in_specs=[anyspec] * 4,
        out_specs=[keys, keys, vals, vals],
        out_shape=[jax.ShapeDtypeStruct(a.shape, a.dtype) for a in (ks, kw, vst, vwt)],
        input_output_aliases={0: 0, 1: 1, 2: 2, 3: 3},
        compiler_params=_params("parallel"),
        name="pad_tiles",
    )(ks, kw, vst, vwt)


def _in_proj(x, g, w_a, w_bt):
    B, S, _ = x.shape
    tm = ROW_TILE
    nq = tm // QBLK
    full = lambda a: pl.BlockSpec(a.shape, lambda b, s: (0,) * a.ndim)
    seq = lambda n: pl.BlockSpec((1, tm, n), lambda b, s: (b, s, 0))
    keys = pl.BlockSpec((1, N_KV, tm, LANES), lambda b, s: (b, 0, s, 0))
    vals = pl.BlockSpec((1, N_KV, VROWS, tm), lambda b, s: (b, 0, 0, s))
    return pl.pallas_call(
        _in_proj_kernel,
        grid=(B, S // tm),
        in_specs=[seq(D_MODEL), full(g), full(w_a), full(w_bt)],
        out_specs=[
            seq(POOL_WIDTH), seq(2 * CONV_WIDTH),
            pl.BlockSpec((1, 2, N_KV, tm, HEAD_DIM), lambda b, s: (b, 0, 0, s, 0)),
            keys, keys,
            pl.BlockSpec((1, N_KV, nq, HEAD_DIM, QROWS), lambda b, s: (b, 0, s, 0, 0)),
            vals, vals,
            pl.BlockSpec((1, nq, N_KV, Q_PER_KV * N_BRANCH, QBLK), lambda b, s: (b, s, 0, 0, 0)),
        ],
        out_shape=[
            jax.ShapeDtypeStruct((B, S, POOL_WIDTH), F32),
            jax.ShapeDtypeStruct((B, S, 2 * CONV_WIDTH), F32),
            jax.ShapeDtypeStruct((B, 2, N_KV, S, HEAD_DIM), F32),
            jax.ShapeDtypeStruct((B, N_KV, S + KTILE, LANES), BF16),
            jax.ShapeDtypeStruct((B, N_KV, S + KTILE, LANES), BF16),
            jax.ShapeDtypeStruct((B, N_KV, S // QBLK, HEAD_DIM, QROWS), BF16),
            jax.ShapeDtypeStruct((B, N_KV, VROWS, S + KTILE), BF16),
            jax.ShapeDtypeStruct((B, N_KV, VROWS, S + KTILE), BF16),
            jax.ShapeDtypeStruct((B, S // QBLK, N_KV, Q_PER_KV * N_BRANCH, QBLK), F32),
        ],
        compiler_params=_params("parallel", "parallel"),
        name="in_proj",
    )(x, g, w_a, w_bt)


def _mixers_kernel(up_ref, uc_ref, pw_ref, ps_ref, cw_ref, cb_ref, lg_ref, lb_ref, cpw_ref,
                   yp_ref, yc_ref, ubuf, hbuf):
    s = pl.program_id(1)
    ts = up_ref.shape[1]

    @pl.when(s == 0)
    def _():
        ubuf[0:HALO, :] = jnp.zeros((HALO, POOL_WIDTH), F32)
        hbuf[0:HALO, :] = jnp.zeros((HALO, CONV_WIDTH), F32)

    ext = HALO + ts
    back = lambda a, k: a + pltpu.roll(a, k, 0)

    u = up_ref[0]
    ubuf[HALO:ext, :] = u
    xl = ubuf[:, 0:128]
    xh = ubuf[:, 128:256]
    s2l = back(xl, 1)
    s4l = back(s2l, 2)
    s8h = back(back(back(xh, 1), 2), 4)
    s16h = back(s8h, 8)
    t1 = (s * ts + lax.broadcasted_iota(jnp.int32, (ts, 128), 0) + 1).astype(F32)
    lane = lax.broadcasted_iota(jnp.int32, (ts, 128), 1)
    first = lane < POOL_GROUP
    sum_lo = jnp.where(first, s2l[HALO:ext], s4l[HALO:ext])
    den_lo = jnp.minimum(t1, jnp.where(first, 2.0, 4.0))
    sum_hi = jnp.where(first, s8h[HALO:ext], s16h[HALO:ext])
    den_hi = jnp.minimum(t1, jnp.where(first, 8.0, 16.0))
    d = jnp.concatenate([sum_lo / den_lo - u[:, 0:128], sum_hi / den_hi - u[:, 128:256]], axis=1)
    yp_ref[0] = (_dot(d.astype(BF16), pw_ref[...]) * ps_ref[...]).astype(BF16)
    ubuf[0:HALO, :] = ubuf[ts:ext, :]

    uc = uc_ref[0]
    hbuf[HALO:ext, :] = uc[:, 0:CONV_WIDTH] * jax.nn.sigmoid(uc[:, CONV_WIDTH:])
    halves = []
    for c0 in range(0, CONV_WIDTH, LANES):
        hx = hbuf[:, c0:c0 + LANES]
        ahead = [hx] + [pltpu.roll(hx, ext - r, 0) for r in range(1, 8)]
        acc_h = jnp.zeros((ts, LANES), F32) + cb_ref[:, c0:c0 + LANES]
        for j in range(CONV_KSIZE):
            a8, r = divmod(HALO - (CONV_KSIZE - 1) + j, 8)
            acc_h = acc_h + ahead[r][8 * a8:8 * a8 + ts] * cw_ref[j:j + 1, c0:c0 + LANES]
        halves.append(acc_h)
    acc = jnp.concatenate(halves, axis=1)
    mu = jnp.mean(acc, axis=-1, keepdims=True)
    var = jnp.mean(jnp.square(acc - mu), axis=-1, keepdims=True)
    y = (acc - mu) * lax.rsqrt(var + EPS) * lg_ref[...] + lb_ref[...]
    y = y * jax.nn.sigmoid(y)
    yc_ref[0] = _dot(y.astype(BF16), cpw_ref[...]).astype(BF16)
    hbuf[0:HALO, :] = hbuf[ts:ts + HALO, :]


def _mixers(up, uc, pw, ps, cw, cb, lg, lb, cpw):
    B, S, _ = up.shape
    ts = ROW_TILE
    seq = lambda n: pl.BlockSpec((1, ts, n), lambda b, s: (b, s, 0))
    full = lambda a: pl.BlockSpec(a.shape, lambda b, s: (0,) * a.ndim)
    return pl.pallas_call(
        _mixers_kernel,
        grid=(B, S // ts),
        in_specs=[seq(256), seq(512)] + [full(a) for a in (pw, ps, cw, cb, lg, lb, cpw)],
        out_specs=[seq(256), seq(256)],
        out_shape=[jax.ShapeDtypeStruct((B, S, 256), BF16)] * 2,
        scratch_shapes=[pltpu.VMEM((HALO + ts, POOL_WIDTH), F32), pltpu.VMEM((HALO + ts, CONV_WIDTH), F32)],
        compiler_params=_params("parallel", "arbitrary"),
        name="mixers",
    )(up, uc, pw, ps, cw, cb, lg, lb, cpw)


def _compress_kernel(r_ref, pos_ref, w1_ref, w2k_ref, w2vt_ref, kc_ref, vct_ref):
    ncp = r_ref.shape[3] // CMP_STRIDE
    half = CMP_STRIDE * HEAD_DIM

    def hidden(kv):
        r = jnp.concatenate([r_ref[0, kv, 0, pl.ds(l, ncp, stride=CMP_STRIDE), :] for l in range(CMP_STRIDE)],
                            axis=1)
        a = _dot((r + pos_ref[kv, 0]).astype(BF16), w1_ref[kv, 0:half, :])
        b = _dot((r + pos_ref[kv, 1]).astype(BF16), w1_ref[kv, half:2 * half, :])
        return jax.nn.gelu(a + pltpu.roll(b, ncp - 1, 0)).astype(BF16)

    kc = _dot(hidden(0), w2k_ref[...]).astype(BF16)
    kc_ref[0, 0] = jnp.concatenate([kc, jnp.zeros((ncp, LANES - HEAD_DIM), BF16)], axis=1)
    ones_row = jnp.where(lax.broadcasted_iota(jnp.int32, (VROWS - HEAD_DIM, ncp), 0) == 0, 1.0, 0.0).astype(BF16)
    vct_ref[0, 0] = jnp.concatenate([_dot_nt(w2vt_ref[...], hidden(1)).astype(BF16), ones_row], axis=0)


def _compress(r, pos, w1, w2k, w2vt):
    B, _, G, S, _ = r.shape
    ncp = S // CMP_STRIDE
    full = lambda a: pl.BlockSpec(a.shape, lambda b, g: (0,) * a.ndim)
    return pl.pallas_call(
        _compress_kernel,
        grid=(B, G),
        in_specs=[pl.BlockSpec((1, 2, 1, S, HEAD_DIM), lambda b, g: (b, 0, g, 0, 0)),
                  full(pos), full(w1), full(w2k), full(w2vt)],
        out_specs=[pl.BlockSpec((1, 1, ncp, LANES), lambda b, g: (b, g, 0, 0)),
                   pl.BlockSpec((1, 1, VROWS, ncp), lambda b, g: (b, g, 0, 0))],
        out_shape=[jax.ShapeDtypeStruct((B, G, ncp, LANES), BF16),
                   jax.ShapeDtypeStruct((B, G, VROWS, ncp), BF16)],
        compiler_params=_params("parallel", "parallel"),
        name="compress",
    )(r, pos, w1, w2k, w2vt)


def _rel_bucket_np(dist):
    n = np.maximum(dist, 0)
    nf = np.maximum(n, 1).astype(np.float64)
    large = MAX_EXACT + (np.log(nf / MAX_EXACT) / math.log(MAX_DISTANCE / MAX_EXACT)
                         * (N_BUCKETS - MAX_EXACT)).astype(np.int64)
    return np.where(n < MAX_EXACT, n, np.minimum(large, N_BUCKETS - 1)).astype(np.int32)


def _static_maps():
    qq = np.arange(QBLK)[:, None]
    kk = np.arange(WINDOW + QBLK)[None, :]
    dist = qq + WINDOW - kk
    wmap = np.where((dist >= 0) & (dist < WINDOW), _rel_bucket_np(dist), -1)
    u = np.arange(LANES)[None, :]
    dist_c = qq - CMP_STRIDE * (u - CNEAR_LO) - (CMP_LEN - 1)
    cmap = np.where((dist_c >= 0) & (u < CNEAR), _rel_bucket_np(dist_c), -1)
    cmap = np.where(u < CNEAR, cmap, -2)
    return wmap.astype(np.int32), cmap.astype(np.int32)


def _tables_kernel(bias_ref, wmap_ref, cmap_ref, tw_ref, tc_ref):
    h = pl.program_id(0)
    far = bias_ref[N_BUCKETS - 1, h]

    def lookup(m):
        out = jnp.where(m == -1, NEG_INF, 0.0).astype(F32)
        for b in range(N_BUCKETS):
            out = jnp.where(m == b, (bias_ref[b, h] - far) * LOG2E, out)
        return out

    tw_ref[0] = lookup(wmap_ref[...])
    cmap = cmap_ref[...]
    val = lookup(cmap)
    hi = val.astype(BF16)
    lo = (val - hi.astype(F32)).astype(BF16)
    col = lax.broadcasted_iota(jnp.int32, cmap.shape, 1)
    hi_part = jnp.where(col < CNEAR, hi.astype(F32), 0.0)
    lo_part = pltpu.roll(jnp.where(col < CNEAR, lo.astype(F32), 0.0), CNEAR, 1)
    one = jnp.where(col == 2 * CNEAR, 1.0, 0.0)
    tc_ref[0] = (hi_part + lo_part + one).astype(BF16)


def _bias_tables(rel_bias):
    wmap, cmap = _static_maps()
    nw = WINDOW + QBLK
    return pl.pallas_call(
        _tables_kernel,
        grid=(N_HEADS,),
        in_specs=[
            pl.BlockSpec(memory_space=pltpu.SMEM),
            pl.BlockSpec((QBLK, nw), lambda h: (0, 0)),
            pl.BlockSpec((QBLK, LANES), lambda h: (0, 0)),
        ],
        out_specs=[
            pl.BlockSpec((1, QBLK, nw), lambda h: (h, 0, 0)),
            pl.BlockSpec((1, QBLK, LANES), lambda h: (h, 0, 0)),
        ],
        out_shape=[
            jax.ShapeDtypeStruct((N_HEADS, QBLK, nw), F32),
            jax.ShapeDtypeStruct((N_HEADS, QBLK, LANES), BF16),
        ],
        compiler_params=_params("parallel"),
        name="bias_tables",
    )(rel_bias, jnp.asarray(wmap), jnp.asarray(cmap))


def _nsa_tables(S):
    ncp = S // CMP_STRIDE
    nb = LANES
    pos = np.arange(S + KTILE)
    eblk = (pos[:, None] // SEL_BLOCK == np.arange(nb)[None, :]) & (pos[:, None] < S)
    cs = np.arange(ncp)[None, :] * CMP_STRIDE
    ss = np.arange(nb)[:, None] * SEL_BLOCK
    ovlt = (cs < ss + SEL_BLOCK) & (cs + CMP_LEN > ss) & (np.arange(ncp)[None, :] < ncp - 1) & (ss < S)
    rel = np.arange(2 * ncp)[:, None] - ncp + CNEAR_LO
    u = np.arange(LANES)[None, :]
    place = np.where((rel == u) & (u < CNEAR), 1.0, 0.0)
    place = np.where((rel == u - CNEAR) & (u >= CNEAR) & (u < 2 * CNEAR), 1.0, place)
    place = np.where((u == 2 * CNEAR) & (rel >= CNEAR), NEG_BF16, place)
    return jnp.asarray(eblk, BF16), jnp.asarray(ovlt, BF16), jnp.asarray(place, BF16)


def _nsa_kernel(qt_ref, kc_ref, vc_ref, eb_ref, ks_ref, vs_ref, kw_ref, vw_ref, gate_ref, tw_ref, tc_ref,
                ovl_ref, place_ref, o_ref, qaug_ref, s0_ref, s1_ref, m_ref, acc_ref, oc_next_ref, msel_next_ref):
    i = pl.program_id(2)
    ncp = kc_ref.shape[2]
    nb = ovl_ref.shape[0]
    pad_tile = ks_ref.shape[2] // KTILE - 1
    col_max = lambda a: jnp.max(a, axis=0, keepdims=True)
    prob = lambda a: jnp.exp2(a).astype(BF16)
    tile_off = lambda t, ok: pl.multiple_of(jnp.where(ok, t, pad_tile) * KTILE, KTILE)

    def query_operand(q_blk):
        row = lax.broadcasted_iota(jnp.int32, (LANES - HEAD_DIM, QROWS), 0)
        return jnp.concatenate([q_blk, jnp.where(row == 0, NEG_BF16, 0.0).astype(BF16)], axis=0)

    def select_operands(q_blk, blk):
        place = place_ref[pl.ds(pl.multiple_of(ncp - CMP_STRIDE * blk, CMP_STRIDE), ncp), :]
        return place, jnp.concatenate([query_operand(q_blk), tc_ref[0]], axis=0)

    def cmp_logits(place, rhs, c0, rows):
        return _dot(jnp.concatenate([kc_ref[0, 0, c0:c0 + rows, :], place[c0:c0 + rows]], axis=1), rhs)

    def cmp_values(c0, rows):
        return jnp.concatenate([vc_ref[0, 0, :, c0:c0 + rows], ovl_ref[:, c0:c0 + rows]], axis=0)

    def stage(state, s, vt):
        if state is None:
            m = jnp.maximum(col_max(s), M_INIT)
            return m, _dot(vt, prob(s - m))
        m_prev, acc = state
        m = jnp.maximum(m_prev, col_max(s))
        return m, jnp.exp2(m_prev - m) * acc + _dot(vt, prob(s - m))

    def select_store(acc, blk):
        inv_l = 1.0 / jnp.maximum(acc[HEAD_DIM:HEAD_DIM + 1, :], 1e-30)
        imp4 = acc[VROWS:VROWS + nb, :] * inv_l
        imp = imp4[:, 0:QBLK] + imp4[:, QBLK:2 * QBLK] + imp4[:, 2 * QBLK:3 * QBLK] + imp4[:, 3 * QBLK:4 * QBLK]
        jb = lax.broadcasted_iota(jnp.int32, (nb, QBLK), 0)
        qq = lax.broadcasted_iota(jnp.int32, (nb, QBLK), 1)
        cur = (QBLK // SEL_BLOCK) * blk + (qq >> 6)
        forced = (jb == 0) | (jb == cur) | (jb == cur - 1)
        sc0 = jnp.where(forced, TAKEN, jnp.where(jb <= cur, imp, NEG_INF))
        n_rounds = N_SELECT - N_FORCED

        def store(sc):
            msel = jnp.where(sc == TAKEN, 0.0, NEG_BF16).astype(BF16)
            msel_next_ref[...] = jnp.concatenate([msel] * Q_PER_KV, axis=1)

        sc = sc0
        for _ in range(n_rounds):
            sc = jnp.where(sc == col_max(sc), TAKEN, sc)
        store(sc)
        oc_next_ref[...] = acc[0:HEAD_DIM, :] * inv_l

        picked = jnp.where((jb <= cur) & jnp.logical_not(forced) & (sc == TAKEN), 1.0, 0.0)
        most = jnp.max(jnp.sum(picked, axis=0, keepdims=True))

        @pl.when(most > n_rounds)
        def _():
            jio = jb.astype(F32)
            sc = sc0
            for _ in range(n_rounds):
                idx = jnp.min(jnp.where(sc == col_max(sc), jio, float(nb)), axis=0, keepdims=True)
                sc = jnp.where(jio == idx, TAKEN, sc)
            store(sc)

    @pl.when(i == 0)
    def _():
        place, rhs = select_operands(qt_ref[0, 0, 0], 0)
        select_store(stage(None, cmp_logits(place, rhs, 0, ncp), cmp_values(0, ncp))[1], 0)

    qn = query_operand(qt_ref[0, 0, i])
    qaug_ref[nb:nb + LANES, :] = qn
    qaug_ref[0:nb, :] = msel_next_ref[...]
    o_c = oc_next_ref[...]

    def sel_logits(o):
        keys = jnp.concatenate([eb_ref[pl.ds(o, KTILE), :], ks_ref[0, 0, pl.ds(o, KTILE), :]], axis=1)
        return _dot(keys, qaug_ref[...])

    n_far = jnp.maximum(i - 1, 0)
    far_off = lambda t: tile_off(t, t < n_far)

    nxt = jnp.minimum(i + 1, pl.num_programs(2) - 1)
    place_n, rhs_n = select_operands(qt_ref[0, 0, nxt], nxt)
    ct = min(KTILE, ncp)
    jobs = [("cmp", ct, functools.partial(cmp_logits, place_n, rhs_n, c0, ct), functools.partial(cmp_values, c0, ct))
            for c0 in range(0, ncp, ct)]
    for tt in range(WINDOW // KTILE + 1):
        t = i - WINDOW // KTILE + tt
        o = tile_off(t, t >= 0)
        jobs.append(("win", KTILE,
                     lambda o=o, tt=tt: (_dot(kw_ref[0, 0, pl.ds(o, KTILE), :], qn)
                                         + tw_ref[0, tt * KTILE:(tt + 1) * KTILE, :]),
                     lambda o=o: vw_ref[0, 0, :, pl.ds(o, KTILE)]))
    for o, t0 in ((tile_off(i - 1, i >= 1), WINDOW - KTILE), (pl.multiple_of(i * KTILE, KTILE), WINDOW)):
        jobs.append(("sel", KTILE, lambda o=o, t0=t0: sel_logits(o) + tw_ref[0, t0:t0 + KTILE, :],
                     lambda o=o: vs_ref[0, 0, :, pl.ds(o, KTILE)]))
    jobs.append(("far", KTILE, lambda: sel_logits(far_off(0)), None))

    bufs = (s0_ref, s1_ref)
    bufs[0][0:jobs[0][1], :] = jobs[0][2]()
    state = {}
    for k, (branch, rows, _, values) in enumerate(jobs[:-1]):
        bufs[(k + 1) % 2][0:jobs[k + 1][1], :] = jobs[k + 1][2]()
        state[branch] = stage(state.get(branch), bufs[k % 2][0:rows, :], values())
    far_bufs = (bufs[(len(jobs) - 1) % 2], bufs[len(jobs) % 2])
    acc_w = state["win"][1]
    m_ref[...] = jnp.broadcast_to(state["sel"][0], m_ref.shape)
    acc_ref[...] = state["sel"][1]
    select_store(state["cmp"][1], nxt)

    def update(s_ref, t):
        o = far_off(t)
        sl = s_ref[...]
        m_prev = m_ref[...]
        m_next = jnp.maximum(m_prev, col_max(sl))
        alpha = jnp.exp2(m_prev - m_next)
        pt = prob(sl - m_next[0:1, :])
        acc_ref[...] = alpha[0:1, :] * acc_ref[...] + _dot(vs_ref[0, 0, :, pl.ds(o, KTILE)], pt)
        m_ref[...] = m_next

    def far_trips(first_tile, n_trips, unroll):
        def body(k, carry):
            for h in range(unroll):
                t = first_tile + unroll * k + h
                far_bufs[(h + 1) % 2][...] = sel_logits(far_off(t + 1))
                update(far_bufs[h % 2], t)
            return carry

        lax.fori_loop(0, n_trips, body, 0)

    done = 0
    for unroll in FAR_UNROLLS[:-1]:
        n_trips = (n_far - done) // unroll
        far_trips(done, n_trips, unroll)
        done = done + n_trips * unroll
    last = FAR_UNROLLS[-1]
    far_trips(done, (n_far - done + last - 1) // last, last)
    acc_s = acc_ref[...]

    gb = gate_ref[0, i, 0]
    gate = lambda br: jnp.concatenate(
        [gb[r * N_BRANCH + br:r * N_BRANCH + br + 1, :] for r in range(Q_PER_KV)], axis=1)
    out = (gate(0) * o_c
           + (gate(1) / acc_s[HEAD_DIM:HEAD_DIM + 1, :]) * acc_s[0:HEAD_DIM, :]
           + (gate(2) / acc_w[HEAD_DIM:HEAD_DIM + 1, :]) * acc_w[0:HEAD_DIM, :]).astype(BF16)
    q_off = pl.multiple_of(i * QBLK, QBLK)
    for r in range(Q_PER_KV):
        o_ref[0, r * HEAD_DIM:(r + 1) * HEAD_DIM, pl.ds(q_off, QBLK)] = out[:, r * QBLK:(r + 1) * QBLK]


def _nsa(qt, kc, vct, eblk, ks, vst, kw, vwt, gates, twt, tct, ovlt, place):
    B, G, NQ, _, _ = qt.shape
    S = NQ * QBLK
    nb = ovlt.shape[0]
    per_bg = lambda a: pl.BlockSpec((1, 1) + a.shape[2:], lambda b, g, i: (b, g) + (0,) * (a.ndim - 2))
    per_g = lambda a: pl.BlockSpec((1,) + a.shape[1:], lambda b, g, i: (g,) + (0,) * (a.ndim - 1))
    const = lambda a: pl.BlockSpec(a.shape, lambda b, g, i: (0,) * a.ndim)
    return pl.pallas_call(
        _nsa_kernel,
        grid=(B, G, NQ),
        in_specs=[per_bg(qt),
                  per_bg(kc), per_bg(vct), const(eblk), per_bg(ks), per_bg(vst), per_bg(kw), per_bg(vwt),
                  pl.BlockSpec((1, NQ, 1, Q_PER_KV * N_BRANCH, QBLK), lambda b, g, i: (b, 0, g, 0, 0)),
                  per_g(twt), per_g(tct), const(ovlt), const(place)],
        out_specs=pl.BlockSpec((1, Q_PER_KV * HEAD_DIM, S), lambda b, g, i: (b, g, 0)),
        out_shape=jax.ShapeDtypeStruct((B, NSA_WIDTH, S), BF16),
        scratch_shapes=[pltpu.VMEM((nb + LANES, QROWS), BF16),
                        pltpu.VMEM((KTILE, QROWS), F32), pltpu.VMEM((KTILE, QROWS), F32),
                        pltpu.VMEM((8, QROWS), F32), pltpu.VMEM((VROWS, QROWS), F32),
                        pltpu.VMEM((HEAD_DIM, QROWS), F32), pltpu.VMEM((nb, QROWS), BF16)],
        compiler_params=_params("parallel", "parallel", "arbitrary"),
        name="nsa",
    )(qt, kc, vct, eblk, ks, vst, kw, vwt, gates, twt, tct, ovlt, place)


def _mem_kv_kernel(mem_ref, g_ref, wk_ref, wvt_ref, k_ref, vt_ref):
    m = _rms(mem_ref[0], g_ref[...]).astype(BF16)
    k_ref[0] = _dot(m, wk_ref[...]).astype(BF16)
    vt_ref[0] = _dot_nt(wvt_ref[...], m).astype(BF16)


def _mem_kv(mem, g, wk, wvt):
    B, M, _ = mem.shape
    full = lambda a: pl.BlockSpec(a.shape, lambda b: (0,) * a.ndim)
    per_b = pl.BlockSpec((1, M, D_MODEL), lambda b: (b, 0, 0))
    per_bt = pl.BlockSpec((1, D_MODEL, M), lambda b: (b, 0, 0))
    return pl.pallas_call(
        _mem_kv_kernel,
        grid=(B,),
        in_specs=[per_b, full(g), full(wk), full(wvt)],
        out_specs=[per_b, per_bt],
        out_shape=[jax.ShapeDtypeStruct((B, M, D_MODEL), BF16), jax.ShapeDtypeStruct((B, D_MODEL, M), BF16)],
        compiler_params=_params("parallel"),
        name="mem_kv",
    )(mem, g, wk, wvt)


def _mix_xattn_kernel(x_ref, yp_ref, yc_ref, ynt_ref, wmix_ref, gmix_ref,
                      gpre_ref, wq_ref, k_ref, v_ref, wo_ref, gpost_ref, o_ref):
    ts = x_ref.shape[1]
    subs = [slice(r, r + XA_SUB) for r in range(0, ts, XA_SUB)]
    mix = [_dot(yp_ref[0, r, :], wmix_ref[0:256, :]) + _dot(yc_ref[0, r, :], wmix_ref[256:512, :])
           + lax.dot_general(ynt_ref[0, :, r], wmix_ref[512:1024, :], (((0,), (0,)), ((), ())),
                             preferred_element_type=F32) for r in subs]
    x = [x_ref[0, r, :] + _rms(m, gmix_ref[...]) for r, m in zip(subs, mix)]
    h = [_rms(xr, gpre_ref[...]).astype(BF16) for xr in x]
    qxt = [(_dot_nt(wq_ref[...], hr) * (XA_HEAD_DIM ** -0.5)).astype(BF16) for hr in h]
    attn = []
    for qr in qxt:
        outs = []
        for hh in range(XA_HEADS):
            sl = slice(hh * XA_HEAD_DIM, (hh + 1) * XA_HEAD_DIM)
            s = _dot(k_ref[0, :, sl], qr[sl, :])
            p = jnp.exp(s - jnp.max(s, axis=0, keepdims=True))
            inv_l = 1.0 / jnp.sum(p, axis=0, keepdims=True)
            outs.append((_dot(v_ref[0, sl, :], p.astype(BF16)) * inv_l).astype(BF16))
        attn.append(jnp.concatenate(outs, axis=0))
    for r, xr, ot in zip(subs, x, attn):
        y = lax.dot_general(ot, wo_ref[...], (((0,), (0,)), ((), ())), preferred_element_type=F32)
        o_ref[0, r, :] = xr + _rms(y, gpost_ref[...])


def _mix_xattn(x, yp, yc, ynt, wmix, gmix, gpre, wq, km, vm, wo, gpost):
    B, S, _ = x.shape
    M = km.shape[1]
    ts = XA_TILE
    full = lambda a: pl.BlockSpec(a.shape, lambda b, s: (0,) * a.ndim)
    seq = lambda n: pl.BlockSpec((1, ts, n), lambda b, s: (b, s, 0))
    per_b = pl.BlockSpec((1, M, D_MODEL), lambda b, s: (b, 0, 0))
    per_bt = pl.BlockSpec((1, D_MODEL, M), lambda b, s: (b, 0, 0))
    return pl.pallas_call(
        _mix_xattn_kernel,
        grid=(B, S // ts),
        in_specs=[seq(D_MODEL), seq(POOL_WIDTH), seq(CONV_WIDTH),
                  pl.BlockSpec((1, NSA_WIDTH, ts), lambda b, s: (b, 0, s)), full(wmix), full(gmix),
                  full(gpre), full(wq), per_b, per_bt, full(wo), full(gpost)],
        out_specs=seq(D_MODEL),
        out_shape=jax.ShapeDtypeStruct((B, S, D_MODEL), F32),
        compiler_params=_params("parallel", "parallel"),
        name="mix_xattn",
    )(x, yp, yc, ynt, wmix, gmix, gpre, wq, km, vm, wo, gpost)


def _mlp_kernel(x_ref, gpre_ref, w1_ref, w2_ref, gpost_ref, o_ref):
    x = x_ref[...]
    h = _rms(x, gpre_ref[...]).astype(BF16)
    y = jnp.zeros(x.shape, F32)
    chunk = 1024
    for c in range(D_FF // chunk):
        a = jnp.maximum(_dot(h, w1_ref[:, c * chunk:(c + 1) * chunk]), 0.0)
        y = y + _dot((a * a).astype(BF16), w2_ref[c * chunk:(c + 1) * chunk, :])
    o_ref[...] = x + _rms(y, gpost_ref[...])


def _mlp(xf, gpre, w1, w2, gpost):
    T = xf.shape[0]
    row = pl.BlockSpec((ROW_TILE, D_MODEL), lambda i: (i, 0))
    full = lambda a: pl.BlockSpec(a.shape, lambda i: (0,) * a.ndim)
    return pl.pallas_call(
        _mlp_kernel,
        grid=(T // ROW_TILE,),
        in_specs=[row, full(gpre), full(w1), full(w2), full(gpost)],
        out_specs=row,
        out_shape=jax.ShapeDtypeStruct((T, D_MODEL), F32),
        compiler_params=_params("parallel"),
        name="mlp",
    )(xf, gpre, w1, w2, gpost)


def kernel(x, mem, rel_bias, mix_pre_g, mix_post_g, w_in, pool_w, pool_scale, conv_w, conv_b, conv_ln_g,
           conv_ln_b, conv_pw, cmp_k_pos, cmp_k_w1, cmp_k_w2, cmp_v_pos, cmp_v_w1, cmp_v_w2, w_out,
           xa_pre_g, xa_post_g, mem_g, xa_wq, xa_wk, xa_wv, xa_wo, mlp_pre_g, mlp_post_g, mlp_w1, mlp_w2):
    B, S, _ = x.shape
    assert S % XA_TILE == 0 and S // SEL_BLOCK >= N_SELECT and S // SEL_BLOCK <= LANES
    T = B * S
    row2 = lambda a: a.reshape(1, -1)

    tw, tc = _bias_tables(rel_bias)
    twt = tw.reshape(N_KV, QROWS, WINDOW + QBLK).transpose(0, 2, 1)
    tct = tc.reshape(N_KV, QROWS, LANES).transpose(0, 2, 1)
    eblk, ovlt, place = _nsa_tables(S)

    for l in range(DEPTH):
        w_a, w_bt = _in_proj_weights(w_in[l])
        up, uc, kcr, ks, kw, qt, vst, vwt, gates = _in_proj(x, row2(mix_pre_g[l]), w_a, w_bt)
        ks, kw, vst, vwt = _pad_tiles(ks, kw, vst, vwt)

        pw_bd = jnp.zeros((POOL_WIDTH, POOL_WIDTH), F32)
        for g in range(len(POOL_WINDOWS)):
            pw_bd = lax.dynamic_update_slice(pw_bd, pool_w[l, g], (g * POOL_GROUP, g * POOL_GROUP))
        yp, yc = _mixers(up, uc, pw_bd.astype(BF16), row2(pool_scale[l]), conv_w[l], row2(conv_b[l]),
                         row2(conv_ln_g[l]), row2(conv_ln_b[l]), conv_pw[l].astype(BF16))

        cpos = jnp.stack([cmp_k_pos[l], cmp_v_pos[l]]).reshape(2, 2, 1, CMP_STRIDE * HEAD_DIM)
        cw1 = jnp.stack([cmp_k_w1[l], cmp_v_w1[l]]).astype(BF16)
        kc, vct = _compress(kcr, cpos, cw1, cmp_k_w2[l].astype(BF16), cmp_v_w2[l].T.astype(BF16))

        ynt = _nsa(qt, kc, vct, eblk, ks, vst, kw, vwt, gates, twt, tct, ovlt, place)

        km, vmt = _mem_kv(mem, row2(mem_g[l]), xa_wk[l].astype(BF16), xa_wv[l].T.astype(BF16))
        x = _mix_xattn(x, yp, yc, ynt, w_out[l].astype(BF16), row2(mix_post_g[l]), row2(xa_pre_g[l]),
                       xa_wq[l].T.astype(BF16), km, vmt, xa_wo[l].astype(BF16), row2(xa_post_g[l]))

        x = _mlp(x.reshape(T, D_MODEL), row2(mlp_pre_g[l]), mlp_w1[l].astype(BF16), mlp_w2[l].astype(BF16),
                 row2(mlp_post_g[l])).reshape(B, S, D_MODEL)
    return x
```
